```python
import jax
import jax.numpy as jnp
from jax import lax
import numpy as np

D_MODEL = 1024
BATCH = 2
SEQ = 8192
DEPTH = 4
DEC_BATCH = 32
DEC_SEQ = 8
PAST_LEN = 8192
PAGE_SIZE = 128

ML_HEADS = 4
ML_HEAD_DIM = D_MODEL // 16
ML_WIDTH = ML_HEADS * ML_HEAD_DIM
CONV_DIM = D_MODEL // 4
CONV_WIDTH = 3
SB_HEADS = 8
SB_HEAD_DIM = D_MODEL // 16
SB_WIDTH = SB_HEADS * SB_HEAD_DIM
MIX_WIDTH = ML_WIDTH + CONV_DIM + SB_WIDTH
D_FF = ((8 * D_MODEL // 3 + 127) // 128) * 128
SPLIT_SIZES = (ML_WIDTH, ML_WIDTH, ML_WIDTH, ML_WIDTH, ML_HEADS, ML_HEADS,
               CONV_DIM, CONV_DIM, CONV_DIM, SB_WIDTH, SB_WIDTH, SB_WIDTH)
IN_DIM = sum(SPLIT_SIZES)
F_GATE_OFF = 4 * ML_WIDTH + ML_HEADS
MLSTM_CHUNK = 64
SB_BLOCK = 128
LN_EPS = 1e-5
DN_ALPHA = (2 * DEPTH) ** 0.25
DN_BETA = (8 * DEPTH) ** -0.25

kernel_name = 'hybrid_mlstm_shortconv_stickbreak_step'


def _split_proj(p):
    cuts = []
    acc = 0
    for s in SPLIT_SIZES[:-1]:
        acc += s
        cuts.append(acc)
    return jnp.split(p, cuts, axis=-1)


def layer_norm(x, g, b):
    xf = x.astype(jnp.float32)
    mu = jnp.mean(xf, axis=-1, keepdims=True)
    var = jnp.mean(jnp.square(xf - mu), axis=-1, keepdims=True)
    y = (xf - mu) * lax.rsqrt(var + LN_EPS) * g.astype(jnp.float32) + b.astype(jnp.float32)
    return y.astype(x.dtype)


def causal_conv(x, w, prev):
    t = x.shape[1]
    xp = jnp.concatenate([prev.astype(x.dtype), x], axis=1)
    y = w[0] * xp[:, 0:t]
    for j in range(1, CONV_WIDTH):
        y = y + w[j] * xp[:, j:j + t]
    return y, xp[:, -(CONV_WIDTH - 1):]


def mlstm_chunk(carry, inp):
    c, n, m = carry
    q, k, v, ig, lf = inp
    L = q.shape[1]
    b = jnp.cumsum(lf, axis=1)
    causal = jnp.tril(jnp.ones((L, L), dtype=bool))[None, :, :, None]
    log_d = jnp.where(causal, b[:, :, None, :] - b[:, None, :, :] + ig[:, None, :, :], -jnp.inf)
    m_inter = b + m[:, None, :]
    m_t = jnp.maximum(m_inter, jnp.max(log_d, axis=2))
    d = jnp.exp(log_d - m_t[:, :, None, :])
    inter = jnp.exp(m_inter - m_t)
    w = jnp.einsum('bthd,bshd->btsh', q, k) * d
    num = jnp.einsum('btsh,bshd->bthd', w, v) + inter[..., None] * jnp.einsum('bthd,bhde->bthe', q, c)
    den = jnp.sum(w, axis=2) + inter * jnp.einsum('bthd,bhd->bth', q, n)
    h = num / jnp.maximum(jnp.abs(den), jnp.exp(-m_t))[..., None]
    m_new = m_t[:, -1]
    w_s = jnp.exp(b[:, -1:, :] - b + ig - m_new[:, None, :])
    decay = jnp.exp(b[:, -1] + m - m_new)
    c_new = decay[..., None, None] * c + jnp.einsum('bsh,bshd,bshe->bhde', w_s, k, v)
    n_new = decay[..., None] * n + jnp.einsum('bsh,bshd->bhd', w_s, k)
    return (c_new, n_new, m_new), h


def mlstm_run(q, k, v, ig, lf, c, n, m):
    bsz, t, h, d = q.shape
    chunk = MLSTM_CHUNK if t % MLSTM_CHUNK == 0 else t
    nc = t // chunk

    def to_chunks(a):
        return jnp.moveaxis(a.reshape((bsz, nc, chunk) + a.shape[2:]), 1, 0)

    (c, n, m), hs = lax.scan(mlstm_chunk, (c, n, m),
                             (to_chunks(q), to_chunks(k), to_chunks(v), to_chunks(ig), to_chunks(lf)))
    return jnp.moveaxis(hs, 0, 1).reshape(bsz, t, h, d), c, n, m


def sb_block(q, qpos, k, v, kpos, bias):
    z = jnp.einsum('bqhd,bshd->bhqs', q, k) + bias[None, :, None, None]
    mask = (kpos[None, :] < qpos[:, None])[None, None]
    log_keep = jnp.where(mask, jax.nn.log_sigmoid(-z), 0.0)
    later = lax.cumsum(log_keep, axis=3, reverse=True) - log_keep
    a = jnp.where(mask, jnp.exp(jax.nn.log_sigmoid(z) + later), 0.0)
    return jnp.einsum('bhqs,bshd->bqhd', a, v)


def sb_attention(q, k, v, bias, q_offset):
    bsz, t, h, d = q.shape
    blk = SB_BLOCK if t % SB_BLOCK == 0 else t
    nb = t // blk
    kpos = jnp.arange(k.shape[1])
    qpos = (q_offset + jnp.arange(t)).reshape(nb, blk)
    qb = jnp.moveaxis(q.reshape(bsz, nb, blk, h, d), 1, 0)
    out = lax.map(lambda a: sb_block(a[0], a[1], k, v, kpos, bias), (qb, qpos))
    return jnp.moveaxis(out, 0, 1).reshape(bsz, t, h, d)


def hybrid_layer(x, past_k, past_v, ml_c, ml_n, ml_m, conv_prev, ffn_prev, q_offset,
                 w_in, b_in, ml_norm_w, conv_w, sb_bias, w_out, ln1_g, ln1_b,
                 w_up, ffn_conv_w, w_down, ln2_g, ln2_b):
    bsz, t, _ = x.shape
    dt = x.dtype
    f32 = jnp.float32
    proj = x @ w_in + b_in
    mq, mk, mv, mo, mi, mf, cb, cc, ch, sq, sk, sv = _split_proj(proj)

    q = mq.reshape(bsz, t, ML_HEADS, ML_HEAD_DIM).astype(f32)
    k = mk.reshape(bsz, t, ML_HEADS, ML_HEAD_DIM).astype(f32) * (ML_HEAD_DIM ** -0.5)
    v = mv.reshape(bsz, t, ML_HEADS, ML_HEAD_DIM).astype(f32)
    ig = mi.astype(f32)
    lf = jax.nn.log_sigmoid(mf.astype(f32))
    h, ml_c, ml_n, ml_m = mlstm_run(q, k, v, ig, lf, ml_c.astype(f32), ml_n.astype(f32), ml_m.astype(f32))
    mu = jnp.mean(h, axis=-1, keepdims=True)
    var = jnp.mean(jnp.square(h - mu), axis=-1, keepdims=True)
    hn = ((h - mu) * lax.rsqrt(var + LN_EPS)).reshape(bsz, t, ML_WIDTH)
    h_ml = (hn * ml_norm_w.astype(f32) * jax.nn.sigmoid(mo.astype(f32))).astype(dt)

    u, conv_state = causal_conv(cc * ch, conv_w, conv_prev)
    h_conv = cb * u

    sq4 = sq.reshape(bsz, t, SB_HEADS, SB_HEAD_DIM)
    sk4 = sk.reshape(bsz, t, SB_HEADS, SB_HEAD_DIM)
    sv4 = sv.reshape(bsz, t, SB_HEADS, SB_HEAD_DIM)
    if past_k is None:
        k_all, v_all = sk4, sv4
    else:
        k_all = jnp.concatenate([past_k.astype(dt), sk4], axis=1)
        v_all = jnp.concatenate([past_v.astype(dt), sv4], axis=1)
    h_sb = sb_attention(sq4.astype(f32) * (SB_HEAD_DIM ** -0.5), k_all.astype(f32), v_all.astype(f32),
                        sb_bias.astype(f32), q_offset)
    h_sb = h_sb.astype(dt).reshape(bsz, t, SB_WIDTH)

    mix = jnp.concatenate([h_ml, h_conv, h_sb], axis=-1) @ w_out
    x = layer_norm(DN_ALPHA * x + mix, ln1_g, ln1_b)

    up = x @ w_up
    g_pre, val = jnp.split(up, [D_FF], axis=-1)
    g_conv, ffn_state = causal_conv(g_pre, ffn_conv_w, ffn_prev)
    ff = (jax.nn.silu(g_conv) * val) @ w_down
    x = layer_norm(DN_ALPHA * x + ff, ln2_g, ln2_b)
    return x, sk4, sv4, ml_c.astype(dt), ml_n.astype(dt), ml_m.astype(dt), conv_state, ffn_state


def setup_inputs(seed: int = 0) -> dict:
    key = jax.random.key(seed)
    ks = jax.random.split(key, 24)
    f32 = jnp.float32
    n_pages = PAST_LEN // PAGE_SIZE
    n_used = DEC_BATCH * n_pages
    n_phys = n_used + max(1, n_used // 4)
    nrm = lambda k, s: jax.random.normal(k, s, f32)
    page_table = jax.random.permutation(ks[9], n_phys)[:n_used].reshape(DEC_BATCH, n_pages).astype(jnp.int32)
    b_in = 0.02 * nrm(ks[11], (DEPTH, IN_DIM))
    b_in = b_in.at[:, F_GATE_OFF:F_GATE_OFF + ML_HEADS].add(jnp.linspace(3.0, 6.0, ML_HEADS, dtype=f32))
    sb_bias = jnp.linspace(-5.0, -9.0, SB_HEADS, dtype=f32)[None, :] + 0.1 * nrm(ks[22], (DEPTH, SB_HEADS))
    return {
        'x_prompt': nrm(ks[0], (BATCH, SEQ, D_MODEL)),
        'x_sample': nrm(ks[1], (DEC_BATCH, DEC_SEQ, D_MODEL)),
        'cache_k': nrm(ks[2], (DEPTH, n_phys, PAGE_SIZE, SB_HEADS, SB_HEAD_DIM)),
        'cache_v': nrm(ks[3], (DEPTH, n_phys, PAGE_SIZE, SB_HEADS, SB_HEAD_DIM)),
        'state_mlstm_c': 0.3 * nrm(ks[4], (DEPTH, DEC_BATCH, ML_HEADS, ML_HEAD_DIM, ML_HEAD_DIM)),
        'state_mlstm_n': 0.3 * nrm(ks[5], (DEPTH, DEC_BATCH, ML_HEADS, ML_HEAD_DIM)),
        'state_mlstm_m': nrm(ks[6], (DEPTH, DEC_BATCH, ML_HEADS)),
        'state_conv': nrm(ks[7], (DEPTH, DEC_BATCH, CONV_WIDTH - 1, CONV_DIM)),
        'state_ffn_conv': nrm(ks[8], (DEPTH, DEC_BATCH, CONV_WIDTH - 1, D_FF)),
        'page_table': page_table,
        'w_in': nrm(ks[10], (DEPTH, D_MODEL, IN_DIM)) * D_MODEL ** -0.5,
        'b_in': b_in,
        'mlstm_norm_w': 1.0 + 0.02 * nrm(ks[12], (DEPTH, ML_WIDTH)),
        'conv_w': nrm(ks[13], (DEPTH, CONV_WIDTH, CONV_DIM)) * CONV_WIDTH ** -0.5,
        'sb_bias': sb_bias,
        'w_out': nrm(ks[14], (DEPTH, MIX_WIDTH, D_MODEL)) * (MIX_WIDTH ** -0.5 * DN_BETA),
        'ln1_g': 1.0 + 0.02 * nrm(ks[15], (DEPTH, D_MODEL)),
        'ln1_b': 0.02 * nrm(ks[16], (DEPTH, D_MODEL)),
        'ffn_w_up': nrm(ks[17], (DEPTH, D_MODEL, 2 * D_FF)) * D_MODEL ** -0.5,
        'ffn_conv_w': nrm(ks[18], (DEPTH, CONV_WIDTH, D_FF)) * CONV_WIDTH ** -0.5,
        'ffn_w_down': nrm(ks[19], (DEPTH, D_FF, D_MODEL)) * (D_FF ** -0.5 * DN_BETA),
        'ln2_g': 1.0 + 0.02 * nrm(ks[20], (DEPTH, D_MODEL)),
        'ln2_b': 0.02 * nrm(ks[21], (DEPTH, D_MODEL)),
    }


def reference(x_prompt, x_sample, cache_k, cache_v, state_mlstm_c, state_mlstm_n, state_mlstm_m,
              state_conv, state_ffn_conv, page_table, w_in, b_in, mlstm_norm_w, conv_w, sb_bias, w_out,
              ln1_g, ln1_b, ffn_w_up, ffn_conv_w, ffn_w_down, ln2_g, ln2_b):
    f32 = jnp.float32
    n_pages = PAST_LEN // PAGE_SIZE
    past_len = n_pages * PAGE_SIZE
    bp = x_prompt.shape[0]
    bs = x_sample.shape[0]
    yp, ys = x_prompt, x_sample
    st_p = [[] for _ in range(7)]
    st_s = [[] for _ in range(7)]
    for l in range(DEPTH):
        weights = (w_in[l], b_in[l], mlstm_norm_w[l], conv_w[l], sb_bias[l], w_out[l], ln1_g[l], ln1_b[l],
                   ffn_w_up[l], ffn_conv_w[l], ffn_w_down[l], ln2_g[l], ln2_b[l])
        yp, *new_p = hybrid_layer(
            yp, None, None,
            jnp.zeros((bp, ML_HEADS, ML_HEAD_DIM, ML_HEAD_DIM), f32),
            jnp.zeros((bp, ML_HEADS, ML_HEAD_DIM), f32),
            jnp.zeros((bp, ML_HEADS), f32),
            jnp.zeros((bp, CONV_WIDTH - 1, CONV_DIM), yp.dtype),
            jnp.zeros((bp, CONV_WIDTH - 1, D_FF), yp.dtype),
            0, *weights)
        past_k = cache_k[l][page_table].reshape(bs, past_len, SB_HEADS, SB_HEAD_DIM)
        past_v = cache_v[l][page_table].reshape(bs, past_len, SB_HEADS, SB_HEAD_DIM)
        ys, *new_s = hybrid_layer(
            ys, past_k, past_v, state_mlstm_c[l], state_mlstm_n[l], state_mlstm_m[l],
            state_conv[l], state_ffn_conv[l], past_len, *weights)
        for lst, a in zip(st_p, new_p):
            lst.append(a)
        for lst, a in zip(st_s, new_s):
            lst.append(a)
    k_p, v_p, c_p, n_p, m_p, conv_p, ffn_p = [jnp.stack(s) for s in st_p]
    k_s, v_s, c_s, n_s, m_s, conv_s, ffn_s = [jnp.stack(s) for s in st_s]
    return (yp, ys, k_p, v_p, c_p, n_p, m_p, conv_p, ffn_p, k_s, v_s, c_s, n_s, m_s, conv_s, ffn_s)
```

```python
import functools

import jax
import jax.numpy as jnp
from jax import lax
from jax.experimental import pallas as pl
from jax.experimental.pallas import tpu as pltpu

F32 = jnp.float32
BF16 = jnp.bfloat16

D_MODEL = 1024
DEPTH = 4
ML_HEADS = 4
HEAD_DIM = 64
ML_WIDTH = ML_HEADS * HEAD_DIM
CONV_DIM = 256
CONV_WIDTH = 3
SB_HEADS = 8
SB_WIDTH = SB_HEADS * HEAD_DIM
D_FF = 2816
PAGE_SIZE = 128
LN_EPS = 1e-5
DN_ALPHA = (2 * DEPTH) ** 0.25

LANES = 128
SUBLANES = 8
MIB = 1024 * 1024

C_ML = 0
ML_COLS = ML_HEADS * 3 * LANES
C_MO = C_ML + ML_COLS
C_CV = C_MO + ML_WIDTH
C_SQ = C_CV + 3 * CONV_DIM
C_SK = C_SQ + SB_WIDTH
C_SV = C_SK + SB_WIDTH
C_G = C_SV + SB_WIDTH
PROJ_COLS = C_G + LANES

MLSTM_CHUNK = 256
SAMPLE_PAD = 128
ROW_TILE = 512
ATT_TQ = 512
ATT_TK = 256
PAGES_PER_STEP = 8
FF_CHUNK = 256


def _cparams(sem, vmem_mib):
    return pltpu.CompilerParams(dimension_semantics=sem, vmem_limit_bytes=vmem_mib * MIB)


def _const_spec(shape):
    zeros = (0,) * len(shape)
    return pl.BlockSpec(shape, lambda *_: zeros)


def _softplus(z):
    return jnp.maximum(z, 0.0) + jnp.log(1.0 + jnp.exp(-jnp.abs(z)))


def _layer_norm(y, g, b):
    mu = jnp.mean(y, axis=-1, keepdims=True)
    d = y - mu
    var = jnp.mean(d * d, axis=-1, keepdims=True)
    return d * lax.rsqrt(var + LN_EPS) * g + b


def _proj_kernel(x_ref, w_ref, b_ref, ml_ref, mo_ref, cv_ref, q_ref, kb_ref, vb_ref, k_ref, v_ref, g_ref):
    x = x_ref[...].astype(BF16)

    def mm(c0, c1):
        return jnp.dot(x, w_ref[:, c0:c1], preferred_element_type=F32) + b_ref[:, c0:c1]

    for c in range(0, ML_COLS, 512):
        ml_ref[:, c:c + 512] = mm(C_ML + c, C_ML + c + 512).astype(BF16)
    mo_ref[...] = mm(C_MO, C_MO + ML_WIDTH)
    for c in range(0, 3 * CONV_DIM, CONV_DIM):
        cv_ref[:, c:c + CONV_DIM] = mm(C_CV + c, C_CV + c + CONV_DIM)
    q_ref[...] = mm(C_SQ, C_SQ + SB_WIDTH).astype(BF16)
    k = mm(C_SK, C_SK + SB_WIDTH)
    k_ref[...] = k
    kb_ref[...] = k.astype(BF16)
    v = mm(C_SV, C_SV + SB_WIDTH)
    v_ref[...] = v
    vb_ref[...] = v.astype(BF16)
    g_ref[...] = mm(C_G, C_G + LANES)


def _proj(x, w, b, tm):
    m = x.shape[0]
    row = lambda width: pl.BlockSpec((tm, width), lambda i: (i, 0))
    outs = [(ML_COLS, BF16), (ML_WIDTH, F32), (3 * CONV_DIM, F32), (SB_WIDTH, BF16), (SB_WIDTH, BF16),
            (SB_WIDTH, BF16), (SB_WIDTH, F32), (SB_WIDTH, F32), (LANES, F32)]
    return pl.pallas_call(
        _proj_kernel,
        grid=(m // tm,),
        in_specs=[row(D_MODEL), _const_spec((D_MODEL, PROJ_COLS)), _const_spec((1, PROJ_COLS))],
        out_specs=[row(wd) for wd, _ in outs],
        out_shape=[jax.ShapeDtypeStruct((m, wd), dt) for wd, dt in outs],
        compiler_params=_cparams(("arbitrary",), 56),
        name="proj",
    )(x, w, b)


def _mlstm_kernel(ml_ref, mo_ref, g_ref, c0_ref, m0_ref, nw_ref, up_ref, h_ref, c_ref, m_ref, cst, mst,
                  *, chunk, valid_len):
    L = chunk

    @pl.when(pl.program_id(1) == 0)
    def _():
        cst[...] = c0_ref[0]
        mst[...] = m0_ref[0]

    gt = g_ref[...].T
    t_ig = gt[0:SUBLANES, :]
    t_f = gt[SUBLANES:2 * SUBLANES, :]
    lf = -_softplus(-t_f)
    if valid_len < L:
        pos = lax.broadcasted_iota(jnp.int32, (SUBLANES, L), 1)
        t_ig = jnp.where(pos < valid_len, t_ig, -1e30)
        lf = jnp.where(pos < valid_len, lf, 0.0)
    hi = lf.astype(BF16)
    r1 = lf - hi.astype(F32)
    mid = r1.astype(BF16)
    lo = (r1 - mid.astype(F32)).astype(BF16)
    up = up_ref[...]
    b_row = (jnp.dot(hi, up, preferred_element_type=F32) + jnp.dot(mid, up, preferred_element_type=F32)
             + jnp.dot(lo, up, preferred_element_type=F32))
    a_row = t_ig - b_row
    cols = jnp.concatenate([a_row, b_row, jnp.zeros((LANES - 2 * SUBLANES, L), F32)], axis=0).T

    row_i = lax.broadcasted_iota(jnp.int32, (L, L), 0)
    col_i = lax.broadcasted_iota(jnp.int32, (L, L), 1)
    causal = row_i >= col_i
    lane_m = lax.broadcasted_iota(jnp.int32, (1, LANES), 1)
    m_all = mst[...]
    m_next = m_all

    for h in range(ML_HEADS):
        base = 3 * LANES * h
        q = ml_ref[:, base:base + LANES]
        k = ml_ref[:, base + LANES:base + 2 * LANES]
        v = ml_ref[:, base + 2 * LANES:base + 3 * LANES]
        a_c = cols[:, h:h + 1]
        b_c = cols[:, SUBLANES + h:SUBLANES + h + 1]
        a_r = a_row[h:h + 1, :]
        m_prev = m_all[:, h:h + 1]
        c_h = cst[h]

        log_d = jnp.where(causal, b_c + a_r, -jnp.inf)
        m_loc = jnp.max(log_d, axis=1, keepdims=True)
        m_inter = b_c + m_prev
        m_t = jnp.maximum(m_inter, m_loc)
        d = jnp.exp(log_d - m_t)
        inter = jnp.exp(m_inter - m_t)
        s = lax.dot_general(q, k, (((1,), (1,)), ((), ())), preferred_element_type=F32)
        w = (s * d).astype(BF16)
        nd = (jnp.dot(w, v, preferred_element_type=F32)
              + inter * jnp.dot(q, c_h.astype(BF16), preferred_element_type=F32))
        num = nd[:, 0:HEAD_DIM]
        den = nd[:, HEAD_DIM:HEAD_DIM + 1]
        hh = num / jnp.maximum(jnp.abs(den), jnp.exp(-m_t))
        mu = jnp.mean(hh, axis=-1, keepdims=True)
        dh = hh - mu
        var = jnp.mean(dh * dh, axis=-1, keepdims=True)
        o = mo_ref[:, HEAD_DIM * h:HEAD_DIM * (h + 1)]
        gate = 1.0 / (1.0 + jnp.exp(-o))
        hn = dh * lax.rsqrt(var + LN_EPS) * nw_ref[:, HEAD_DIM * h:HEAD_DIM * (h + 1)] * gate
        h_ref[:, HEAD_DIM * h:HEAD_DIM * (h + 1)] = hn.astype(BF16)

        m_new = m_t[L - 1:L, :]
        b_last = b_c[L - 1:L, :]
        w_s = jnp.exp(b_last + a_c - m_new)
        decay = jnp.exp(b_last + m_prev - m_new)
        kw = (k.astype(F32) * w_s).astype(BF16)
        cst[h] = decay * c_h + lax.dot_general(kw, v, (((0,), (0,)), ((), ())), preferred_element_type=F32)
        m_next = jnp.where(lane_m == h, m_new, m_next)

    mst[...] = m_next
    c_ref[0] = cst[...]
    m_ref[0] = m_next


def _mlstm(ml, mo, g, c0, m0, nw, nseq, chunk, valid_len):
    m = ml.shape[0]
    nc = m // nseq // chunk
    up = (lax.broadcasted_iota(jnp.int32, (chunk, chunk), 0)
          <= lax.broadcasted_iota(jnp.int32, (chunk, chunk), 1)).astype(BF16)
    row = lambda width: pl.BlockSpec((chunk, width), lambda b, j: (b * nc + j, 0))
    st_c = pl.BlockSpec((1, ML_HEADS, LANES, LANES), lambda b, j: (b, 0, 0, 0))
    st_m = pl.BlockSpec((1, 1, LANES), lambda b, j: (b, 0, 0))
    return pl.pallas_call(
        functools.partial(_mlstm_kernel, chunk=chunk, valid_len=valid_len),
        grid=(nseq, nc),
        in_specs=[row(ML_COLS), row(ML_WIDTH), row(LANES), st_c, st_m, _const_spec((1, ML_WIDTH)),
                  _const_spec((chunk, chunk))],
        out_specs=[row(ML_WIDTH), st_c, st_m],
        out_shape=[jax.ShapeDtypeStruct((m, ML_WIDTH), BF16),
                   jax.ShapeDtypeStruct((nseq, ML_HEADS, LANES, LANES), F32),
                   jax.ShapeDtypeStruct((nseq, 1, LANES), F32)],
        scratch_shapes=[pltpu.VMEM((ML_HEADS, LANES, LANES), F32), pltpu.VMEM((1, LANES), F32)],
        compiler_params=_cparams(("arbitrary", "arbitrary"), 32),
        name="mlstm",
    )(ml, mo, g, c0, m0, nw, up)


def _sb_prompt_kernel(bias_ref, q_ref, k_ref, v_ref, tri_ref, o_ref, acc_ref, carry_ref, *, tq, tk):
    hp = pl.program_id(1)
    i = pl.program_id(2)
    nkb = tq // tk
    q = q_ref[...]
    lane = lax.broadcasted_iota(jnp.int32, (tq, LANES), 1)
    tri = tri_ref[...]

    def step(j, qm, bias, masked):
        ks = pl.multiple_of(j * tk, tk)
        kb = k_ref[pl.ds(ks, tk), :]
        vb = v_ref[pl.ds(ks, tk), :]
        z = lax.dot_general(qm, kb, (((1,), (1,)), ((), ())), preferred_element_type=F32) + bias
        sp = _softplus(z)
        if masked:
            qpos = i * tq + lax.broadcasted_iota(jnp.int32, (tq, tk), 0)
            kpos = j * tk + lax.broadcasted_iota(jnp.int32, (tq, tk), 1)
            keep = kpos < qpos
            sp = jnp.where(keep, sp, 0.0)
        cum = jnp.dot(sp.astype(BF16), tri, preferred_element_type=F32)
        a = jnp.exp(z + cum + carry_ref[...])
        if masked:
            a = jnp.where(keep, a, 0.0)
        carry_ref[...] += cum[:, 0:1]
        acc_ref[...] += jnp.dot(a.astype(BF16), vb, preferred_element_type=F32)

    halves = []
    for hh in range(2):
        own = (lane >= HEAD_DIM) if hh else (lane < HEAD_DIM)
        qm = jnp.where(own, q, jnp.zeros_like(q))
        bias = bias_ref[2 * hp + hh]
        acc_ref[...] = jnp.zeros_like(acc_ref)
        carry_ref[...] = jnp.zeros_like(carry_ref)
        for dblk in reversed(range(nkb)):
            step(i * nkb + dblk, qm, bias, True)

        def body(t, _):
            step(i * nkb - 1 - t, qm, bias, False)
            return 0

        lax.fori_loop(0, i * nkb, body, 0)
        halves.append(acc_ref[...])
    o_ref[...] = jnp.where(lane >= HEAD_DIM, halves[1], halves[0]).astype(BF16)


def _sb_prompt(q, kb, vb, bias, nseq, tq, tk):
    m = q.shape[0]
    t = m // nseq
    nq = t // tq
    tri = -(lax.broadcasted_iota(jnp.int32, (tk, tk), 0)
            >= lax.broadcasted_iota(jnp.int32, (tk, tk), 1)).astype(BF16)
    grid_spec = pltpu.PrefetchScalarGridSpec(
        num_scalar_prefetch=1,
        grid=(nseq, SB_HEADS // 2, nq),
        in_specs=[pl.BlockSpec((tq, LANES), lambda b, hp, i, s: (b * nq + i, hp)),
                  pl.BlockSpec((t, LANES), lambda b, hp, i, s: (b, hp)),
                  pl.BlockSpec((t, LANES), lambda b, hp, i, s: (b, hp)),
                  pl.BlockSpec((tk, tk), lambda b, hp, i, s: (0, 0))],
        out_specs=pl.BlockSpec((tq, LANES), lambda b, hp, i, s: (b * nq + i, hp)),
        scratch_shapes=[pltpu.VMEM((tq, LANES), F32), pltpu.VMEM((tq, 1), F32)],
    )
    return pl.pallas_call(
        functools.partial(_sb_prompt_kernel, tq=tq, tk=tk),
        grid_spec=grid_spec,
        out_shape=jax.ShapeDtypeStruct((m, SB_WIDTH), BF16),
        compiler_params=_cparams(("arbitrary", "arbitrary", "arbitrary"), 40),
        name="sb_prompt",
    )(bias, q, kb, vb, tri)


def _sb_sample_kernel(pt_ref, w_ref, brow_ref, kn_ref, vn_ref, tri_ref, *refs, pps, nq):
    k_refs = refs[0:pps]
    v_refs = refs[pps:2 * pps]
    o_ref = refs[2 * pps]
    acc_ref, carry_ref = refs[2 * pps + 1:]
    j = pl.program_id(1)
    w = w_ref[0]
    brow = brow_ref[...]
    tri = tri_ref[...]

    def block(kf, vf, keep):
        z = jnp.dot(kf.astype(BF16), w, preferred_element_type=F32) + brow
        sp = _softplus(z)
        if keep is not None:
            sp = jnp.where(keep, sp, 0.0)
        cum = jnp.dot(tri, sp.astype(BF16), preferred_element_type=F32)
        a = jnp.exp(z + cum + carry_ref[...])
        if keep is not None:
            a = jnp.where(keep, a, 0.0)
        carry_ref[...] += cum[0:1, :]
        acc_ref[...] += lax.dot_general(a.astype(BF16), vf.astype(BF16), (((0,), (0,)), ((), ())),
                                        preferred_element_type=F32)

    @pl.when(j == 0)
    def _():
        acc_ref[...] = jnp.zeros_like(acc_ref)
        carry_ref[...] = jnp.zeros_like(carry_ref)
        s_i = lax.broadcasted_iota(jnp.int32, (PAGE_SIZE, LANES), 0)
        t_i = lax.broadcasted_iota(jnp.int32, (PAGE_SIZE, LANES), 1) % nq
        block(kn_ref[0], vn_ref[0], s_i < t_i)

    for p in range(pps):
        block(k_refs[p][...], v_refs[p][...], None)

    @pl.when(j == pl.num_programs(1) - 1)
    def _():
        acc = acc_ref[...]
        lane_h = lax.broadcasted_iota(jnp.int32, (nq, SB_WIDTH), 1) // HEAD_DIM
        res = jnp.zeros((nq, SB_WIDTH), F32)
        for h in range(SB_HEADS):
            res = jnp.where(lane_h == h, acc[nq * h:nq * (h + 1), :], res)
        o_ref[0] = res


def _sb_sample(wq, brow, k_new, v_new, cache_k, cache_v, page_table, layer, nq):
    bsz, n_pages = page_table.shape
    pps = PAGES_PER_STEP
    steps = n_pages // pps
    tri = -(lax.broadcasted_iota(jnp.int32, (PAGE_SIZE, PAGE_SIZE), 1)
            >= lax.broadcasted_iota(jnp.int32, (PAGE_SIZE, PAGE_SIZE), 0)).astype(BF16)

    def page_spec(p):
        def imap(b, j, pt):
            return (layer, pt[b * n_pages + (n_pages - 1 - (j * pps + p))], 0, 0)
        return pl.BlockSpec((None, None, PAGE_SIZE, SB_WIDTH), imap)

    per_seq = lambda shape: pl.BlockSpec((1,) + shape, lambda b, j, pt: (b, 0, 0))
    grid_spec = pltpu.PrefetchScalarGridSpec(
        num_scalar_prefetch=1,
        grid=(bsz, steps),
        in_specs=[per_seq((SB_WIDTH, LANES)), pl.BlockSpec((1, LANES), lambda b, j, pt: (0, 0)),
                  per_seq((PAGE_SIZE, SB_WIDTH)), per_seq((PAGE_SIZE, SB_WIDTH)),
                  pl.BlockSpec((PAGE_SIZE, PAGE_SIZE), lambda b, j, pt: (0, 0))]
                 + [page_spec(p) for p in range(pps)] + [page_spec(p) for p in range(pps)],
        out_specs=per_seq((nq, SB_WIDTH)),
        scratch_shapes=[pltpu.VMEM((LANES, SB_WIDTH), F32), pltpu.VMEM((1, LANES), F32)],
    )
    return pl.pallas_call(
        functools.partial(_sb_sample_kernel, pps=pps, nq=nq),
        grid_spec=grid_spec,
        out_shape=jax.ShapeDtypeStruct((bsz, nq, SB_WIDTH), F32),
        compiler_params=_cparams(("arbitrary", "arbitrary"), 40),
        name="sb_sample",
    )(page_table.reshape(-1), wq, brow, k_new, v_new, tri, *([cache_k] * pps), *([cache_v] * pps))


def _conv3(p, s_ref, cs, w_ref, tm, halo):
    s_ref[SUBLANES:SUBLANES + tm, cs] = p
    p1 = s_ref[SUBLANES - 1:SUBLANES - 1 + tm, cs]
    p2 = s_ref[SUBLANES - 2:SUBLANES - 2 + tm, cs]
    if halo is not None:
        e1, e2, seq_len = halo
        tpos = lax.broadcasted_iota(jnp.int32, p.shape, 0) % seq_len
        p1 = jnp.where(tpos == 0, e1, p1)
        p2 = jnp.where(tpos < 2, e2, p2)
    return w_ref[0:1, cs] * p2 + w_ref[1:2, cs] * p1 + w_ref[2:3, cs] * p


def _start_tile(s_ref, tiles_per_seq):
    @pl.when(pl.program_id(0) % tiles_per_seq == 0)
    def _():
        s_ref[0:SUBLANES, :] = jnp.zeros((SUBLANES, s_ref.shape[1]), F32)


def _carry_rows(s_ref, tm):
    s_ref[0:SUBLANES, :] = s_ref[tm:tm + SUBLANES, :]


def _outproj_kernel(*refs, tm, tiles_per_seq, seq_len):
    sample = seq_len < tm
    if sample:
        hml_ref, cv_ref, hsb_ref, x_ref, w_ref, cw_ref, g_ref, b_ref, e1_ref, e2_ref, y_ref, tail_ref, s_ref = refs
        halo = (e1_ref[...], e2_ref[...], seq_len)
    else:
        hml_ref, cv_ref, hsb_ref, x_ref, w_ref, cw_ref, g_ref, b_ref, y_ref, tail_ref, s_ref = refs
        halo = None
    _start_tile(s_ref, tiles_per_seq)
    cb = cv_ref[:, 0:CONV_DIM]
    p = cv_ref[:, CONV_DIM:2 * CONV_DIM] * cv_ref[:, 2 * CONV_DIM:3 * CONV_DIM]
    u = _conv3(p, s_ref, slice(None), cw_ref, tm, halo)
    if sample:
        tail_ref[...] = p
    else:
        tail_ref[...] = p[tm - SUBLANES:tm, :]
        _carry_rows(s_ref, tm)
    h_conv = (cb * u).astype(BF16)
    mix = (jnp.dot(hml_ref[...], w_ref[0:ML_WIDTH, :], preferred_element_type=F32)
           + jnp.dot(h_conv, w_ref[ML_WIDTH:ML_WIDTH + CONV_DIM, :], preferred_element_type=F32)
           + jnp.dot(hsb_ref[...].astype(BF16), w_ref[ML_WIDTH + CONV_DIM:, :], preferred_element_type=F32))
    y_ref[...] = _layer_norm(DN_ALPHA * x_ref[...] + mix, g_ref[...], b_ref[...])


def _outproj(hml, cv, hsb, x, w, cw, g, b, tm, seq_len, halo=None):
    m = x.shape[0]
    sample = seq_len < tm
    tiles_per_seq = max(seq_len // tm, 1)
    row = lambda width: pl.BlockSpec((tm, width), lambda i: (i, 0))
    tail_rows = tm if sample else SUBLANES
    in_specs = [row(ML_WIDTH), row(3 * CONV_DIM), row(SB_WIDTH), row(D_MODEL),
                _const_spec((D_MODEL, D_MODEL)), _const_spec((SUBLANES, CONV_DIM)),
                _const_spec((1, D_MODEL)), _const_spec((1, D_MODEL))]
    args = [hml, cv, hsb, x, w, cw, g, b]
    if sample:
        in_specs += [row(CONV_DIM), row(CONV_DIM)]
        args += list(halo)
    return pl.pallas_call(
        functools.partial(_outproj_kernel, tm=tm, tiles_per_seq=tiles_per_seq, seq_len=seq_len),
        grid=(m // tm,),
        in_specs=in_specs,
        out_specs=[row(D_MODEL), pl.BlockSpec((tail_rows, CONV_DIM), lambda i: (i, 0))],
        out_shape=[jax.ShapeDtypeStruct((m, D_MODEL), F32),
                   jax.ShapeDtypeStruct((m // tm * tail_rows, CONV_DIM), F32)],
        scratch_shapes=[pltpu.VMEM((tm + SUBLANES, CONV_DIM), F32)],
        compiler_params=_cparams(("arbitrary",), 40),
        name="outproj",
    )(*args)


def _ffn_kernel(*refs, tm, tiles_per_seq, seq_len):
    sample = seq_len < tm
    if sample:
        x_ref, wu_ref, cw_ref, wd_ref, g_ref, b_ref, e1_ref, e2_ref, y_ref, tail_ref, s_ref, acc_ref = refs
    else:
        x_ref, wu_ref, cw_ref, wd_ref, g_ref, b_ref, y_ref, tail_ref, s_ref, acc_ref = refs
    _start_tile(s_ref, tiles_per_seq)
    x = x_ref[...]
    xb = x.astype(BF16)
    for c in range(D_FF // FF_CHUNK):
        cs = slice(c * FF_CHUNK, (c + 1) * FF_CHUNK)
        g_pre = jnp.dot(xb, wu_ref[:, cs], preferred_element_type=F32)
        val = jnp.dot(xb, wu_ref[:, D_FF + c * FF_CHUNK:D_FF + (c + 1) * FF_CHUNK], preferred_element_type=F32)
        halo = (e1_ref[:, cs], e2_ref[:, cs], seq_len) if sample else None
        g_conv = _conv3(g_pre, s_ref, cs, cw_ref, tm, halo)
        hid = (g_conv / (1.0 + jnp.exp(-g_conv)) * val).astype(BF16)
        part = jnp.dot(hid, wd_ref[cs, :], preferred_element_type=F32)
        if c == 0:
            acc_ref[...] = part
        else:
            acc_ref[...] += part
    if sample:
        tail_ref[...] = s_ref[SUBLANES:SUBLANES + tm, :]
    else:
        tail_ref[...] = s_ref[tm:tm + SUBLANES, :]
        _carry_rows(s_ref, tm)
    y_ref[...] = _layer_norm(DN_ALPHA * x + acc_ref[...], g_ref[...], b_ref[...])


def _ffn(x, wu, cw, wd, g, b, tm, seq_len, halo=None):
    m = x.shape[0]
    sample = seq_len < tm
    tiles_per_seq = max(seq_len // tm, 1)
    row = lambda width: pl.BlockSpec((tm, width), lambda i: (i, 0))
    tail_rows = tm if sample else SUBLANES
    in_specs = [row(D_MODEL),
                pl.BlockSpec((D_MODEL, 2 * D_FF), lambda i: (0, 0), pipeline_mode=pl.Buffered(1)),
                _const_spec((SUBLANES, D_FF)),
                pl.BlockSpec((D_FF, D_MODEL), lambda i: (0, 0), pipeline_mode=pl.Buffered(1)),
                _const_spec((1, D_MODEL)), _const_spec((1, D_MODEL))]
    args = [x, wu, cw, wd, g, b]
    if sample:
        in_specs += [row(D_FF), row(D_FF)]
        args += list(halo)
    return pl.pallas_call(
        functools.partial(_ffn_kernel, tm=tm, tiles_per_seq=tiles_per_seq, seq_len=seq_len),
        grid=(m // tm,),
        in_specs=in_specs,
        out_specs=[row(D_MODEL), pl.BlockSpec((tail_rows, D_FF), lambda i: (i, 0))],
        out_shape=[jax.ShapeDtypeStruct((m, D_MODEL), F32),
                   jax.ShapeDtypeStruct((m // tm * tail_rows, D_FF), F32)],
        scratch_shapes=[pltpu.VMEM((tm + SUBLANES, D_FF), F32), pltpu.VMEM((tm, D_MODEL), F32)],
        compiler_params=_cparams(("arbitrary",), 56),
        name="ffn",
    )(*args)


def _pack_w_in(w_in, b_in):
    depth = w_in.shape[0]
    scale = HEAD_DIM ** -0.5
    zw = lambda n: jnp.zeros((depth, D_MODEL, n), F32)
    zb = lambda n: jnp.zeros((depth, n), F32)
    wp, bp = [], []
    for h in range(ML_HEADS):
        lo, hi = HEAD_DIM * h, HEAD_DIM * (h + 1)
        wp += [w_in[:, :, lo:hi], zw(HEAD_DIM)]
        bp += [b_in[:, lo:hi], zb(HEAD_DIM)]
        wp += [w_in[:, :, ML_WIDTH + lo:ML_WIDTH + hi] * scale, zw(HEAD_DIM)]
        bp += [b_in[:, ML_WIDTH + lo:ML_WIDTH + hi] * scale, zb(HEAD_DIM)]
        wp += [w_in[:, :, 2 * ML_WIDTH + lo:2 * ML_WIDTH + hi], zw(HEAD_DIM)]
        bp += [b_in[:, 2 * ML_WIDTH + lo:2 * ML_WIDTH + hi], jnp.ones((depth, 1), F32), zb(HEAD_DIM - 1)]
    gi = 4 * ML_WIDTH
    cv0 = gi + 2 * ML_HEADS
    sq0 = cv0 + 3 * CONV_DIM
    wp += [w_in[:, :, 3 * ML_WIDTH:4 * ML_WIDTH], w_in[:, :, cv0:sq0],
           w_in[:, :, sq0:sq0 + SB_WIDTH] * scale, w_in[:, :, sq0 + SB_WIDTH:],
           w_in[:, :, gi:gi + ML_HEADS], zw(SUBLANES - ML_HEADS),
           w_in[:, :, gi + ML_HEADS:gi + 2 * ML_HEADS], zw(LANES - SUBLANES - ML_HEADS)]
    bp += [b_in[:, 3 * ML_WIDTH:4 * ML_WIDTH], b_in[:, cv0:sq0],
           b_in[:, sq0:sq0 + SB_WIDTH] * scale, b_in[:, sq0 + SB_WIDTH:],
           b_in[:, gi:gi + ML_HEADS], zb(SUBLANES - ML_HEADS),
           b_in[:, gi + ML_HEADS:gi + 2 * ML_HEADS], zb(LANES - SUBLANES - ML_HEADS)]
    w = jnp.concatenate(wp, axis=-1).astype(BF16)
    b = jnp.concatenate(bp, axis=-1)[:, None, :]
    assert w.shape[-1] == PROJ_COLS and b.shape[-1] == PROJ_COLS
    return w, b


def _pad_rows8(a):
    return jnp.pad(a, ((0, 0), (0, SUBLANES - a.shape[1]), (0, 0)))


def _ext_state(c, n):
    ext = jnp.concatenate([c, n[..., None]], axis=-1)
    return jnp.pad(ext, ((0, 0), (0, 0), (0, LANES - HEAD_DIM), (0, LANES - HEAD_DIM - 1)))


def _halo(prev, seq_len):
    bsz, _, c = prev.shape
    z = jnp.zeros((bsz, seq_len, c), F32)
    e1 = z.at[:, 0].set(prev[:, 1])
    e2 = z.at[:, 0].set(prev[:, 0]).at[:, 1].set(prev[:, 1])
    return e1.reshape(bsz * seq_len, c), e2.reshape(bsz * seq_len, c)


def _layer_prompt(x, nseq, wts):
    (w_in, b_in, nw, cw, sbias, w_out, g1, b1, w_up, fcw, w_down, g2, b2) = wts
    m = x.shape[0]
    t = m // nseq
    ml, mo, cv, sq, kb, vb, sk, sv, g = _proj(x, w_in, b_in, ROW_TILE)
    c0 = jnp.zeros((nseq, ML_HEADS, LANES, LANES), F32)
    m0 = jnp.zeros((nseq, 1, LANES), F32)
    hml, c_ext, m_new = _mlstm(ml, mo, g, c0, m0, nw, nseq, MLSTM_CHUNK, MLSTM_CHUNK)
    hsb = _sb_prompt(sq, kb, vb, sbias, nseq, ATT_TQ, ATT_TK)
    x1, ptail = _outproj(hml, cv, hsb, x, w_out, cw, g1, b1, ROW_TILE, t)
    x2, gtail = _ffn(x1, w_up, fcw, w_down, g2, b2, ROW_TILE, t)
    last = lambda tail: tail.reshape(nseq, t // ROW_TILE, SUBLANES, -1)[:, -1, SUBLANES - 2:, :]
    states = (sk.reshape(nseq, t, SB_HEADS, HEAD_DIM), sv.reshape(nseq, t, SB_HEADS, HEAD_DIM),
              c_ext[:, :, :HEAD_DIM, :HEAD_DIM], c_ext[:, :, :HEAD_DIM, HEAD_DIM], m_new[:, 0, :ML_HEADS],
              last(ptail), last(gtail))
    return x2, states


def _layer_sample(x, nseq, wts, layer, cache_k, cache_v, page_table, c_prev, n_prev, m_prev, conv_prev, ffn_prev):
    (w_in, b_in, nw, cw, sbias, w_out, g1, b1, w_up, fcw, w_down, g2, b2) = wts
    m = x.shape[0]
    t = m // nseq
    ml, mo, cv, sq, _, _, sk, sv, g = _proj(x, w_in, b_in, m)

    pad = lambda a: jnp.pad(a.reshape(nseq, t, -1), ((0, 0), (0, SAMPLE_PAD - t), (0, 0))).reshape(
        nseq * SAMPLE_PAD, -1)
    m0 = jnp.pad(m_prev, ((0, 0), (0, LANES - ML_HEADS)))[:, None, :]
    hml, c_ext, m_new = _mlstm(pad(ml), pad(mo), pad(g), _ext_state(c_prev, n_prev), m0, nw, nseq, SAMPLE_PAD, t)
    hml = hml.reshape(nseq, SAMPLE_PAD, ML_WIDTH)[:, :t].reshape(m, ML_WIDTH)

    q3 = sq.reshape(nseq, t, SB_HEADS, HEAD_DIM)
    eye = jnp.eye(SB_HEADS, dtype=BF16)
    wq = jnp.einsum("bthd,hg->bhdgt", q3, eye).reshape(nseq, SB_WIDTH, SB_HEADS * t)
    wq = jnp.pad(wq, ((0, 0), (0, 0), (0, LANES - SB_HEADS * t)))
    brow = jnp.pad(jnp.repeat(sbias, t), (0, LANES - SB_HEADS * t))[None, :]
    padk = lambda a: jnp.pad(a.reshape(nseq, t, SB_WIDTH), ((0, 0), (0, PAGE_SIZE - t), (0, 0)))
    hsb = _sb_sample(wq, brow, padk(sk), padk(sv), cache_k, cache_v, page_table, layer, t).reshape(m, SB_WIDTH)

    x1, p_all = _outproj(hml, cv, hsb, x, w_out, cw, g1, b1, m, t, _halo(conv_prev, t))
    x2, g_all = _ffn(x1, w_up, fcw, w_down, g2, b2, m, t, _halo(ffn_prev, t))
    last = lambda a: a.reshape(nseq, t, -1)[:, t - 2:, :]
    states = (sk.reshape(nseq, t, SB_HEADS, HEAD_DIM), sv.reshape(nseq, t, SB_HEADS, HEAD_DIM),
              c_ext[:, :, :HEAD_DIM, :HEAD_DIM], c_ext[:, :, :HEAD_DIM, HEAD_DIM], m_new[:, 0, :ML_HEADS],
              last(p_all), last(g_all))
    return x2, states


def kernel(x_prompt, x_sample, cache_k, cache_v, state_mlstm_c, state_mlstm_n, state_mlstm_m, state_conv,
           state_ffn_conv, page_table, w_in, b_in, mlstm_norm_w, conv_w, sb_bias, w_out, ln1_g, ln1_b, ffn_w_up,
           ffn_conv_w, ffn_w_down, ln2_g, ln2_b):
    bp, tp, _ = x_prompt.shape
    bs, ts, _ = x_sample.shape
    depth = w_in.shape[0]
    w_in_p, b_in_p = _pack_w_in(w_in, b_in)
    w_out_b = w_out.astype(BF16)
    w_up_b = ffn_w_up.astype(BF16)
    w_down_b = ffn_w_down.astype(BF16)
    cw_p = _pad_rows8(conv_w)
    fcw_p = _pad_rows8(ffn_conv_w)
    ck = cache_k.reshape(cache_k.shape[0], cache_k.shape[1], PAGE_SIZE, SB_WIDTH)
    cvv = cache_v.reshape(cache_v.shape[0], cache_v.shape[1], PAGE_SIZE, SB_WIDTH)

    yp = x_prompt.reshape(bp * tp, D_MODEL)
    ys = x_sample.reshape(bs * ts, D_MODEL)
    st_p = [[] for _ in range(7)]
    st_s = [[] for _ in range(7)]
    for l in range(depth):
        wts = (w_in_p[l], b_in_p[l], mlstm_norm_w[l][None, :], cw_p[l], sb_bias[l], w_out_b[l],
               ln1_g[l][None, :], ln1_b[l][None, :], w_up_b[l], fcw_p[l], w_down_b[l],
               ln2_g[l][None, :], ln2_b[l][None, :])
        yp, new_p = _layer_prompt(yp, bp, wts)
        ys, new_s = _layer_sample(ys, bs, wts, l, ck, cvv, page_table, state_mlstm_c[l], state_mlstm_n[l],
                                  state_mlstm_m[l], state_conv[l], state_ffn_conv[l])
        for lst, a in zip(st_p, new_p):
            lst.append(a)
        for lst, a in zip(st_s, new_s):
            lst.append(a)
    outs_p = [jnp.stack(s) for s in st_p]
    outs_s = [jnp.stack(s) for s in st_s]
    return (yp.reshape(bp, tp, D_MODEL), ys.reshape(bs, ts, D_MODEL), *outs_p, *outs_s)
```

```python
import functools

import jax
import jax.numpy as jnp
from jax import lax
from jax.experimental import pallas as pl
from jax.experimental.pallas import tpu as pltpu

F32 = jnp.float32
BF16 = jnp.bfloat16

D_MODEL = 1024
DEPTH = 4
ML_HEADS = 4
HEAD_DIM = 64
ML_WIDTH = ML_HEADS * HEAD_DIM
CONV_DIM = 256
CONV_WIDTH = 3
SB_HEADS = 8
SB_WIDTH = SB_HEADS * HEAD_DIM
D_FF = 2816
PAGE_SIZE = 128
LN_EPS = 1e-5
DN_ALPHA = (2 * DEPTH) ** 0.25

LANES = 128
SUBLANES = 8
MIB = 1024 * 1024

C_ML = 0
ML_COLS = ML_HEADS * 3 * LANES
C_MO = C_ML + ML_COLS
C_CV = C_MO + ML_WIDTH
SB_AUG = SB_HEADS * LANES
C_QA = C_CV + 3 * CONV_DIM
C_KA = C_QA + SB_AUG
C_SK = C_KA + SB_AUG
C_SV = C_SK + SB_WIDTH
C_G = C_SV + SB_WIDTH
PROJ_COLS = C_G + LANES
LOG2E = 1.4426950408889634

MLSTM_CHUNK = 256
SAMPLE_PAD = 128
ROW_TILE = 512
ATT_TQ = 512
ATT_TK = 256
PAGES_PER_STEP = 8
FF_CHUNK = 256


def _cparams(sem, vmem_mib):
    return pltpu.CompilerParams(dimension_semantics=sem, vmem_limit_bytes=vmem_mib * MIB)


def _const_spec(shape):
    zeros = (0,) * len(shape)
    return pl.BlockSpec(shape, lambda *_: zeros)


def _softplus(z):
    return jnp.maximum(z, 0.0) + jnp.log(1.0 + jnp.exp(-jnp.abs(z)))


def _softplus2(z):
    sign_bit = jnp.uint32(0x80000000)
    neg_abs = lax.bitcast_convert_type(lax.bitcast_convert_type(z, jnp.uint32) | sign_bit, F32)
    return jnp.maximum(z, 0.0) + jnp.log2(1.0 + jnp.exp2(neg_abs))


def _layer_norm(y, g, b):
    mu = jnp.mean(y, axis=-1, keepdims=True)
    d = y - mu
    var = jnp.mean(d * d, axis=-1, keepdims=True)
    return d * lax.rsqrt(var + LN_EPS) * g + b


def _proj_kernel(x_ref, w_ref, b_ref, wt_ref, bt_ref, ml_ref, mo_ref, cv_ref, qa_ref, ka_ref, vb_ref, k_ref, v_ref,
                 g_ref, *, kv_transposed):
    x = x_ref[...].astype(BF16)

    def mm(c0, c1):
        return jnp.dot(x, w_ref[:, c0:c1], preferred_element_type=F32) + b_ref[:, c0:c1]

    for c in range(0, ML_COLS, 512):
        ml_ref[:, c:c + 512] = mm(C_ML + c, C_ML + c + 512).astype(BF16)
    mo_ref[...] = mm(C_MO, C_MO + ML_WIDTH)
    for c in range(0, 3 * CONV_DIM, CONV_DIM):
        cv_ref[:, c:c + CONV_DIM] = mm(C_CV + c, C_CV + c + CONV_DIM)
    for c in range(0, SB_AUG, 512):
        qa_ref[:, c:c + 512] = mm(C_QA + c, C_QA + c + 512).astype(BF16)
        ka_ref[:, c:c + 512] = mm(C_KA + c, C_KA + c + 512).astype(BF16)
    v = mm(C_SV, C_SV + SB_WIDTH)
    vb_ref[...] = v.astype(BF16)
    if kv_transposed:
        for r, ref in ((0, k_ref), (SB_WIDTH, v_ref)):
            ref[...] = lax.dot_general(wt_ref[r:r + SB_WIDTH, :], x, (((1,), (1,)), ((), ())),
                                       preferred_element_type=F32) + bt_ref[r:r + SB_WIDTH, :]
    else:
        k_ref[...] = mm(C_SK, C_SK + SB_WIDTH)
        v_ref[...] = v
    g_ref[...] = mm(C_G, C_G + LANES)


def _proj(x, w, b, wt, bt, tm, nseq, kv_transposed):
    m = x.shape[0]
    t = m // nseq
    row = lambda width: pl.BlockSpec((tm, width), lambda i: (i, 0))
    outs = [(ML_COLS, BF16), (ML_WIDTH, F32), (3 * CONV_DIM, F32), (SB_AUG, BF16), (SB_AUG, BF16),
            (SB_WIDTH, BF16), (SB_WIDTH, F32), (SB_WIDTH, F32), (LANES, F32)]
    out_specs = [row(wd) for wd, _ in outs]
    out_shape = [jax.ShapeDtypeStruct((m, wd), dt) for wd, dt in outs]
    if kv_transposed:
        tiles = t // tm
        for idx in (6, 7):
            out_specs[idx] = pl.BlockSpec((None, SB_WIDTH, tm), lambda i: (i // tiles, 0, i % tiles))
            out_shape[idx] = jax.ShapeDtypeStruct((nseq, SB_WIDTH, t), F32)
    return pl.pallas_call(
        functools.partial(_proj_kernel, kv_transposed=kv_transposed),
        grid=(m // tm,),
        in_specs=[row(D_MODEL),
                  pl.BlockSpec((D_MODEL, PROJ_COLS), lambda i: (0, 0), pipeline_mode=pl.Buffered(1)),
                  _const_spec((1, PROJ_COLS)),
                  _const_spec((2 * SB_WIDTH, D_MODEL)), _const_spec((2 * SB_WIDTH, 1))],
        out_specs=out_specs,
        out_shape=out_shape,
        compiler_params=_cparams(("arbitrary",), 56),
        name="proj",
    )(x, w, b, wt, bt)


def _mlstm_kernel(ml_ref, mo_ref, g_ref, c0_ref, m0_ref, nw_ref, up_ref, h_ref, c_ref, m_ref, cst, mst,
                  *, chunk, valid_len):
    L = chunk

    @pl.when(pl.program_id(1) == 0)
    def _():
        cst[...] = c0_ref[0]
        mst[...] = m0_ref[0]

    gt = g_ref[...].T
    t_ig = gt[0:SUBLANES, :]
    t_f = gt[SUBLANES:2 * SUBLANES, :]
    lf = -_softplus(-t_f)
    if valid_len < L:
        pos = lax.broadcasted_iota(jnp.int32, (SUBLANES, L), 1)
        t_ig = jnp.where(pos < valid_len, t_ig, -1e30)
        lf = jnp.where(pos < valid_len, lf, 0.0)
    hi = lf.astype(BF16)
    r1 = lf - hi.astype(F32)
    mid = r1.astype(BF16)
    lo = (r1 - mid.astype(F32)).astype(BF16)
    up = up_ref[...]
    b_row = (jnp.dot(hi, up, preferred_element_type=F32) + jnp.dot(mid, up, preferred_element_type=F32)
             + jnp.dot(lo, up, preferred_element_type=F32))
    a_row = t_ig - b_row
    cols = jnp.concatenate([a_row, b_row, jnp.zeros((LANES - 2 * SUBLANES, L), F32)], axis=0).T

    row_i = lax.broadcasted_iota(jnp.int32, (L, L), 0)
    col_i = lax.broadcasted_iota(jnp.int32, (L, L), 1)
    causal = row_i >= col_i
    lane_m = lax.broadcasted_iota(jnp.int32, (1, LANES), 1)
    m_all = mst[...]
    m_next = m_all

    for h in range(ML_HEADS):
        base = 3 * LANES * h
        q = ml_ref[:, base:base + LANES]
        k = ml_ref[:, base + LANES:base + 2 * LANES]
        v = ml_ref[:, base + 2 * LANES:base + 3 * LANES]
        a_c = cols[:, h:h + 1]
        b_c = cols[:, SUBLANES + h:SUBLANES + h + 1]
        a_r = a_row[h:h + 1, :]
        m_prev = m_all[:, h:h + 1]
        c_h = cst[h]

        log_d = jnp.where(causal, b_c + a_r, -jnp.inf)
        m_loc = jnp.max(log_d, axis=1, keepdims=True)
        m_inter = b_c + m_prev
        m_t = jnp.maximum(m_inter, m_loc)
        d = jnp.exp(log_d - m_t)
        inter = jnp.exp(m_inter - m_t)
        s = lax.dot_general(q, k, (((1,), (1,)), ((), ())), preferred_element_type=F32)
        w = (s * d).astype(BF16)
        nd = (jnp.dot(w, v, preferred_element_type=F32)
              + inter * jnp.dot(q, c_h.astype(BF16), preferred_element_type=F32))
        num = nd[:, 0:HEAD_DIM]
        den = nd[:, HEAD_DIM:HEAD_DIM + 1]
        hh = num / jnp.maximum(jnp.abs(den), jnp.exp(-m_t))
        mu = jnp.mean(hh, axis=-1, keepdims=True)
        dh = hh - mu
        var = jnp.mean(dh * dh, axis=-1, keepdims=True)
        o = mo_ref[:, HEAD_DIM * h:HEAD_DIM * (h + 1)]
        gate = 1.0 / (1.0 + jnp.exp(-o))
        hn = dh * lax.rsqrt(var + LN_EPS) * nw_ref[:, HEAD_DIM * h:HEAD_DIM * (h + 1)] * gate
        h_ref[:, HEAD_DIM * h:HEAD_DIM * (h + 1)] = hn.astype(BF16)

        m_new = m_t[L - 1:L, :]
        b_last = b_c[L - 1:L, :]
        w_s = jnp.exp(b_last + a_c - m_new)
        decay = jnp.exp(b_last + m_prev - m_new)
        kw = (k.astype(F32) * w_s).astype(BF16)
        cst[h] = decay * c_h + lax.dot_general(kw, v, (((0,), (0,)), ((), ())), preferred_element_type=F32)
        m_next = jnp.where(lane_m == h, m_new, m_next)

    mst[...] = m_next
    c_ref[0] = cst[...]
    m_ref[0] = m_next


def _mlstm(ml, mo, g, c0, m0, nw, nseq, chunk, valid_len):
    m = ml.shape[0]
    nc = m // nseq // chunk
    up = (lax.broadcasted_iota(jnp.int32, (chunk, chunk), 0)
          <= lax.broadcasted_iota(jnp.int32, (chunk, chunk), 1)).astype(BF16)
    row = lambda width: pl.BlockSpec((chunk, width), lambda b, j: (b * nc + j, 0))
    st_c = pl.BlockSpec((1, ML_HEADS, LANES, LANES), lambda b, j: (b, 0, 0, 0))
    st_m = pl.BlockSpec((1, 1, LANES), lambda b, j: (b, 0, 0))
    return pl.pallas_call(
        functools.partial(_mlstm_kernel, chunk=chunk, valid_len=valid_len),
        grid=(nseq, nc),
        in_specs=[row(ML_COLS), row(ML_WIDTH), row(LANES), st_c, st_m, _const_spec((1, ML_WIDTH)),
                  _const_spec((chunk, chunk))],
        out_specs=[row(ML_WIDTH), st_c, st_m],
        out_shape=[jax.ShapeDtypeStruct((m, ML_WIDTH), BF16),
                   jax.ShapeDtypeStruct((nseq, ML_HEADS, LANES, LANES), F32),
                   jax.ShapeDtypeStruct((nseq, 1, LANES), F32)],
        scratch_shapes=[pltpu.VMEM((ML_HEADS, LANES, LANES), F32), pltpu.VMEM((1, LANES), F32)],
        compiler_params=_cparams(("arbitrary", "arbitrary"), 32),
        name="mlstm",
    )(ml, mo, g, c0, m0, nw, up)


def _sb_prompt_kernel(q_ref, k_ref, v_ref, tri_ref, o_ref, acc_ref, carry_ref, *, tq, tk):
    i = pl.program_id(2)
    nkb = tq // tk
    tri = tri_ref[...]
    acc_ref[...] = jnp.zeros_like(acc_ref)
    carry_ref[...] = jnp.zeros_like(carry_ref)

    def steps(blocks, masked):
        chains = [(n, hh) for n in range(len(blocks)) for hh in range(2)]
        starts = [pl.multiple_of(j * tk, tk) for j in blocks]
        keep = []
        if masked:
            for j in blocks:
                qpos = i * tq + lax.broadcasted_iota(jnp.int32, (tq, tk), 0)
                kpos = j * tk + lax.broadcasted_iota(jnp.int32, (tq, tk), 1)
                keep.append(kpos < qpos)
        z = [lax.dot_general(q_ref[:, hh * LANES:(hh + 1) * LANES],
                             k_ref[pl.ds(starts[j], tk), hh * LANES:(hh + 1) * LANES],
                             (((1,), (1,)), ((), ())), preferred_element_type=F32) for j, hh in chains]
        cum = []
        for c, (j, hh) in enumerate(chains):
            sp = _softplus2(z[c])
            if masked:
                sp = jnp.where(keep[j], sp, 0.0)
            cum.append(jnp.dot(sp.astype(BF16), tri, preferred_element_type=F32))
        pv = []
        for c, (j, hh) in enumerate(chains):
            a = jnp.exp2(z[c] + cum[c])
            if masked:
                a = jnp.where(keep[j], a, 0.0)
            pv.append(jnp.dot(a.astype(BF16), v_ref[pl.ds(starts[j], tk), :], preferred_element_type=F32))
        for c, (j, hh) in enumerate(chains):
            carry = carry_ref[hh]
            acc_ref[hh] += jnp.exp2(carry) * pv[c]
            carry_ref[hh] = carry + cum[c][:, 0:1]

    steps([i * nkb + dblk for dblk in reversed(range(nkb))], True)

    assert nkb % 2 == 0

    def body(t, _):
        steps([i * nkb - 1 - 2 * t, i * nkb - 2 - 2 * t], False)
        return 0

    lax.fori_loop(0, i * (nkb // 2), body, 0)
    lane = lax.broadcasted_iota(jnp.int32, (tq, LANES), 1)
    o_ref[...] = jnp.where(lane >= HEAD_DIM, acc_ref[1], acc_ref[0]).astype(BF16)


def _sb_prompt(qa, ka, vb, nseq, tq, tk):
    m = qa.shape[0]
    t = m // nseq
    nq = t // tq
    tri = -(lax.broadcasted_iota(jnp.int32, (tk, tk), 0)
            >= lax.broadcasted_iota(jnp.int32, (tk, tk), 1)).astype(BF16)
    return pl.pallas_call(
        functools.partial(_sb_prompt_kernel, tq=tq, tk=tk),
        grid=(nseq, SB_HEADS // 2, nq),
        in_specs=[pl.BlockSpec((tq, 2 * LANES), lambda b, hp, i: (b * nq + i, hp)),
                  pl.BlockSpec((t, 2 * LANES), lambda b, hp, i: (b, hp)),
                  pl.BlockSpec((t, LANES), lambda b, hp, i: (b, hp)),
                  pl.BlockSpec((tk, tk), lambda b, hp, i: (0, 0))],
        out_specs=pl.BlockSpec((tq, LANES), lambda b, hp, i: (b * nq + i, hp)),
        out_shape=jax.ShapeDtypeStruct((m, SB_WIDTH), BF16),
        scratch_shapes=[pltpu.VMEM((2, tq, LANES), F32), pltpu.VMEM((2, tq, 1), F32)],
        compiler_params=_cparams(("arbitrary", "arbitrary", "arbitrary"), 40),
        name="sb_prompt",
    )(qa, ka, vb, tri)


def _sb_sample_kernel(pt_ref, qbd_ref, bias_ref, kn_ref, vn_ref, tri_ref, *refs, pps, nq):
    k_refs = refs[0:pps]
    v_refs = refs[pps:2 * pps]
    o_ref = refs[2 * pps]
    acc_ref, carry_ref = refs[2 * pps + 1:]
    j = pl.program_id(1)
    qbd = qbd_ref[0]
    bias = bias_ref[...]
    tri = tri_ref[...]

    def blocks(kts, vts, keep):
        z = [jnp.dot(qbd, kt[...].astype(BF16), preferred_element_type=F32) + bias for kt in kts]
        cum = []
        for zc in z:
            sp = _softplus2(zc)
            if keep is not None:
                sp = jnp.where(keep, sp, 0.0)
            cum.append(jnp.dot(sp.astype(BF16), tri, preferred_element_type=F32))
        pv = []
        for zc, cc, vt in zip(z, cum, vts):
            a = jnp.exp2(zc + cc)
            if keep is not None:
                a = jnp.where(keep, a, 0.0)
            pv.append(lax.dot_general(a.astype(BF16), vt[...].astype(BF16), (((1,), (1,)), ((), ())),
                                      preferred_element_type=F32))
        acc = acc_ref[...]
        carry = carry_ref[...]
        for pc, cc in zip(pv, cum):
            acc = acc + jnp.exp2(carry) * pc
            carry = carry + cc[:, 0:1]
        acc_ref[...] = acc
        carry_ref[...] = carry

    @pl.when(j == 0)
    def _():
        acc_ref[...] = jnp.zeros_like(acc_ref)
        carry_ref[...] = jnp.zeros_like(carry_ref)
        t_i = lax.broadcasted_iota(jnp.int32, (SB_HEADS * nq, PAGE_SIZE), 0) % nq
        s_i = lax.broadcasted_iota(jnp.int32, (SB_HEADS * nq, PAGE_SIZE), 1)
        blocks([kn_ref.at[0]], [vn_ref.at[0]], s_i < t_i)

    blocks(k_refs, v_refs, None)

    @pl.when(j == pl.num_programs(1) - 1)
    def _():
        acc = acc_ref[...]
        lane_h = lax.broadcasted_iota(jnp.int32, (nq, SB_WIDTH), 1) // HEAD_DIM
        res = jnp.zeros((nq, SB_WIDTH), F32)
        for h in range(SB_HEADS):
            res = jnp.where(lane_h == h, acc[nq * h:nq * (h + 1), :], res)
        o_ref[0] = res


def _sb_sample(qbd, bias, kt_new, vt_new, cache_kt, cache_vt, page_table, layer, nq):
    bsz, n_pages = page_table.shape
    pps = PAGES_PER_STEP
    steps = n_pages // pps
    rows = SB_HEADS * nq
    tri = -(lax.broadcasted_iota(jnp.int32, (PAGE_SIZE, PAGE_SIZE), 0)
            >= lax.broadcasted_iota(jnp.int32, (PAGE_SIZE, PAGE_SIZE), 1)).astype(BF16)

    def page_spec(p):
        def imap(b, j, pt):
            return (layer, pt[b * n_pages + (n_pages - 1 - (j * pps + p))], 0, 0)
        return pl.BlockSpec((None, None, SB_WIDTH, PAGE_SIZE), imap)

    per_seq = lambda shape: pl.BlockSpec((1,) + shape, lambda b, j, pt: (b, 0, 0))
    grid_spec = pltpu.PrefetchScalarGridSpec(
        num_scalar_prefetch=1,
        grid=(bsz, steps),
        in_specs=[per_seq((rows, SB_WIDTH)), pl.BlockSpec((rows, PAGE_SIZE), lambda b, j, pt: (0, 0)),
                  per_seq((SB_WIDTH, PAGE_SIZE)), per_seq((SB_WIDTH, PAGE_SIZE)),
                  pl.BlockSpec((PAGE_SIZE, PAGE_SIZE), lambda b, j, pt: (0, 0))]
                 + [page_spec(p) for p in range(pps)] + [page_spec(p) for p in range(pps)],
        out_specs=per_seq((nq, SB_WIDTH)),
        scratch_shapes=[pltpu.VMEM((rows, SB_WIDTH), F32), pltpu.VMEM((rows, 1), F32)],
    )
    return pl.pallas_call(
        functools.partial(_sb_sample_kernel, pps=pps, nq=nq),
        grid_spec=grid_spec,
        out_shape=jax.ShapeDtypeStruct((bsz, nq, SB_WIDTH), F32),
        compiler_params=_cparams(("arbitrary", "arbitrary"), 40),
        name="sb_sample",
    )(page_table.reshape(-1), qbd, bias, kt_new, vt_new, tri, *([cache_kt] * pps), *([cache_vt] * pps))


def _conv3(p, s_ref, cs, w_ref, tm, halo):
    s_ref[SUBLANES:SUBLANES + tm, cs] = p
    p1 = s_ref[SUBLANES - 1:SUBLANES - 1 + tm, cs]
    p2 = s_ref[SUBLANES - 2:SUBLANES - 2 + tm, cs]
    if halo is not None:
        e1, e2, seq_len = halo
        tpos = lax.broadcasted_iota(jnp.int32, p.shape, 0) % seq_len
        p1 = jnp.where(tpos == 0, e1, p1)
        p2 = jnp.where(tpos < 2, e2, p2)
    return w_ref[0:1, cs] * p2 + w_ref[1:2, cs] * p1 + w_ref[2:3, cs] * p


def _start_tile(s_ref, tiles_per_seq):
    @pl.when(pl.program_id(0) % tiles_per_seq == 0)
    def _():
        s_ref[0:SUBLANES, :] = jnp.zeros((SUBLANES, s_ref.shape[1]), F32)


def _carry_rows(s_ref, tm):
    s_ref[0:SUBLANES, :] = s_ref[tm:tm + SUBLANES, :]


def _outproj_kernel(*refs, tm, tiles_per_seq, seq_len):
    sample = seq_len < tm
    if sample:
        hml_ref, cv_ref, hsb_ref, x_ref, w_ref, cw_ref, g_ref, b_ref, e1_ref, e2_ref, y_ref, tail_ref, s_ref = refs
        halo = (e1_ref[...], e2_ref[...], seq_len)
    else:
        hml_ref, cv_ref, hsb_ref, x_ref, w_ref, cw_ref, g_ref, b_ref, y_ref, tail_ref, s_ref = refs
        halo = None
    _start_tile(s_ref, tiles_per_seq)
    cb = cv_ref[:, 0:CONV_DIM]
    p = cv_ref[:, CONV_DIM:2 * CONV_DIM] * cv_ref[:, 2 * CONV_DIM:3 * CONV_DIM]
    u = _conv3(p, s_ref, slice(None), cw_ref, tm, halo)
    if sample:
        tail_ref[...] = p
    else:
        tail_ref[...] = p[tm - SUBLANES:tm, :]
        _carry_rows(s_ref, tm)
    h_conv = (cb * u).astype(BF16)
    mix = (jnp.dot(hml_ref[...], w_ref[0:ML_WIDTH, :], preferred_element_type=F32)
           + jnp.dot(h_conv, w_ref[ML_WIDTH:ML_WIDTH + CONV_DIM, :], preferred_element_type=F32)
           + jnp.dot(hsb_ref[...].astype(BF16), w_ref[ML_WIDTH + CONV_DIM:, :], preferred_element_type=F32))
    y_ref[...] = _layer_norm(DN_ALPHA * x_ref[...] + mix, g_ref[...], b_ref[...])


def _outproj(hml, cv, hsb, x, w, cw, g, b, tm, seq_len, halo=None):
    m = x.shape[0]
    sample = seq_len < tm
    tiles_per_seq = max(seq_len // tm, 1)
    row = lambda width: pl.BlockSpec((tm, width), lambda i: (i, 0))
    tail_rows = tm if sample else SUBLANES
    in_specs = [row(ML_WIDTH), row(3 * CONV_DIM), row(SB_WIDTH), row(D_MODEL),
                _const_spec((D_MODEL, D_MODEL)), _const_spec((SUBLANES, CONV_DIM)),
                _const_spec((1, D_MODEL)), _const_spec((1, D_MODEL))]
    args = [hml, cv, hsb, x, w, cw, g, b]
    if sample:
        in_specs += [row(CONV_DIM), row(CONV_DIM)]
        args += list(halo)
    return pl.pallas_call(
        functools.partial(_outproj_kernel, tm=tm, tiles_per_seq=tiles_per_seq, seq_len=seq_len),
        grid=(m // tm,),
        in_specs=in_specs,
        out_specs=[row(D_MODEL), pl.BlockSpec((tail_rows, CONV_DIM), lambda i: (i, 0))],
        out_shape=[jax.ShapeDtypeStruct((m, D_MODEL), F32),
                   jax.ShapeDtypeStruct((m // tm * tail_rows, CONV_DIM), F32)],
        scratch_shapes=[pltpu.VMEM((tm + SUBLANES, CONV_DIM), F32)],
        compiler_params=_cparams(("arbitrary",), 40),
        name="outproj",
    )(*args)


def _ffn_kernel(*refs, tm, tiles_per_seq, seq_len):
    sample = seq_len < tm
    if sample:
        x_ref, wu_ref, cw_ref, wd_ref, g_ref, b_ref, e1_ref, e2_ref, y_ref, tail_ref, s_ref, acc_ref = refs
    else:
        x_ref, wu_ref, cw_ref, wd_ref, g_ref, b_ref, y_ref, tail_ref, s_ref, acc_ref = refs
    _start_tile(s_ref, tiles_per_seq)
    x = x_ref[...]
    xb = x.astype(BF16)
    for c in range(D_FF // FF_CHUNK):
        cs = slice(c * FF_CHUNK, (c + 1) * FF_CHUNK)
        g_pre = jnp.dot(xb, wu_ref[:, cs], preferred_element_type=F32)
        val = jnp.dot(xb, wu_ref[:, D_FF + c * FF_CHUNK:D_FF + (c + 1) * FF_CHUNK], preferred_element_type=F32)
        halo = (e1_ref[:, cs], e2_ref[:, cs], seq_len) if sample else None
        g_conv = _conv3(g_pre, s_ref, cs, cw_ref, tm, halo)
        hid = (g_conv / (1.0 + jnp.exp(-g_conv)) * val).astype(BF16)
        part = jnp.dot(hid, wd_ref[cs, :], preferred_element_type=F32)
        if c == 0:
            acc_ref[...] = part
        else:
            acc_ref[...] += part
    if sample:
        tail_ref[...] = s_ref[SUBLANES:SUBLANES + tm, :]
    else:
        tail_ref[...] = s_ref[tm:tm + SUBLANES, :]
        _carry_rows(s_ref, tm)
    y_ref[...] = _layer_norm(DN_ALPHA * x + acc_ref[...], g_ref[...], b_ref[...])


def _ffn(x, wu, cw, wd, g, b, tm, seq_len, halo=None):
    m = x.shape[0]
    sample = seq_len < tm
    tiles_per_seq = max(seq_len // tm, 1)
    row = lambda width: pl.BlockSpec((tm, width), lambda i: (i, 0))
    tail_rows = tm if sample else SUBLANES
    in_specs = [row(D_MODEL),
                pl.BlockSpec((D_MODEL, 2 * D_FF), lambda i: (0, 0), pipeline_mode=pl.Buffered(1)),
                _const_spec((SUBLANES, D_FF)),
                pl.BlockSpec((D_FF, D_MODEL), lambda i: (0, 0), pipeline_mode=pl.Buffered(1)),
                _const_spec((1, D_MODEL)), _const_spec((1, D_MODEL))]
    args = [x, wu, cw, wd, g, b]
    if sample:
        in_specs += [row(D_FF), row(D_FF)]
        args += list(halo)
    return pl.pallas_call(
        functools.partial(_ffn_kernel, tm=tm, tiles_per_seq=tiles_per_seq, seq_len=seq_len),
        grid=(m // tm,),
        in_specs=in_specs,
        out_specs=[row(D_MODEL), pl.BlockSpec((tail_rows, D_FF), lambda i: (i, 0))],
        out_shape=[jax.ShapeDtypeStruct((m, D_MODEL), F32),
                   jax.ShapeDtypeStruct((m // tm * tail_rows, D_FF), F32)],
        scratch_shapes=[pltpu.VMEM((tm + SUBLANES, D_FF), F32), pltpu.VMEM((tm, D_MODEL), F32)],
        compiler_params=_cparams(("arbitrary",), 56),
        name="ffn",
    )(*args)


def _split3(x):
    hi = x.astype(BF16).astype(F32)
    mid = (x - hi).astype(BF16).astype(F32)
    lo = (x - hi - mid).astype(BF16).astype(F32)
    return hi, mid, lo


def _pack_w_in(w_in, b_in, sb_bias):
    depth = w_in.shape[0]
    scale = HEAD_DIM ** -0.5
    b2 = _split3(sb_bias * LOG2E)
    zw = lambda n: jnp.zeros((depth, D_MODEL, n), F32)
    zb = lambda n: jnp.zeros((depth, n), F32)
    wp, bp = [], []
    for h in range(ML_HEADS):
        lo, hi = HEAD_DIM * h, HEAD_DIM * (h + 1)
        wp += [w_in[:, :, lo:hi], zw(HEAD_DIM)]
        bp += [b_in[:, lo:hi], zb(HEAD_DIM)]
        wp += [w_in[:, :, ML_WIDTH + lo:ML_WIDTH + hi] * scale, zw(HEAD_DIM)]
        bp += [b_in[:, ML_WIDTH + lo:ML_WIDTH + hi] * scale, zb(HEAD_DIM)]
        wp += [w_in[:, :, 2 * ML_WIDTH + lo:2 * ML_WIDTH + hi], zw(HEAD_DIM)]
        bp += [b_in[:, 2 * ML_WIDTH + lo:2 * ML_WIDTH + hi], jnp.ones((depth, 1), F32), zb(HEAD_DIM - 1)]
    gi = 4 * ML_WIDTH
    cv0 = gi + 2 * ML_HEADS
    sq0 = cv0 + 3 * CONV_DIM
    wp += [w_in[:, :, 3 * ML_WIDTH:4 * ML_WIDTH], w_in[:, :, cv0:sq0]]
    bp += [b_in[:, 3 * ML_WIDTH:4 * ML_WIDTH], b_in[:, cv0:sq0]]
    sk0 = sq0 + SB_WIDTH
    for h in range(SB_HEADS):
        lo, hi = sq0 + HEAD_DIM * h, sq0 + HEAD_DIM * (h + 1)
        wp += [w_in[:, :, lo:hi] * (scale * LOG2E), zw(HEAD_DIM)]
        bp += [b_in[:, lo:hi] * (scale * LOG2E)] + [part[:, h:h + 1] for part in b2] + [zb(HEAD_DIM - 3)]
    for h in range(SB_HEADS):
        lo, hi = sk0 + HEAD_DIM * h, sk0 + HEAD_DIM * (h + 1)
        wp += [w_in[:, :, lo:hi], zw(HEAD_DIM)]
        bp += [b_in[:, lo:hi], jnp.ones((depth, 3), F32), zb(HEAD_DIM - 3)]
    wp += [w_in[:, :, sk0:],
           w_in[:, :, gi:gi + ML_HEADS], zw(SUBLANES - ML_HEADS),
           w_in[:, :, gi + ML_HEADS:gi + 2 * ML_HEADS], zw(LANES - SUBLANES - ML_HEADS)]
    bp += [b_in[:, sk0:],
           b_in[:, gi:gi + ML_HEADS], zb(SUBLANES - ML_HEADS),
           b_in[:, gi + ML_HEADS:gi + 2 * ML_HEADS], zb(LANES - SUBLANES - ML_HEADS)]
    w = jnp.concatenate(wp, axis=-1).astype(BF16)
    b = jnp.concatenate(bp, axis=-1)[:, None, :]
    assert w.shape[-1] == PROJ_COLS and b.shape[-1] == PROJ_COLS
    wt = jnp.swapaxes(w_in[:, :, sk0:], 1, 2).astype(BF16)
    bt = b_in[:, sk0:, None]
    return w, b, wt, bt


def _pad_rows8(a):
    return jnp.pad(a, ((0, 0), (0, SUBLANES - a.shape[1]), (0, 0)))


def _ext_state(c, n):
    ext = jnp.concatenate([c, n[..., None]], axis=-1)
    return jnp.pad(ext, ((0, 0), (0, 0), (0, LANES - HEAD_DIM), (0, LANES - HEAD_DIM - 1)))


def _halo(prev, seq_len):
    bsz, _, c = prev.shape
    z = jnp.zeros((bsz, seq_len, c), F32)
    e1 = z.at[:, 0].set(prev[:, 1])
    e2 = z.at[:, 0].set(prev[:, 0]).at[:, 1].set(prev[:, 1])
    return e1.reshape(bsz * seq_len, c), e2.reshape(bsz * seq_len, c)


def _layer_prompt(x, nseq, wts):
    (w_in, b_in, wt, bt, nw, cw, sbias, w_out, g1, b1, w_up, fcw, w_down, g2, b2) = wts
    m = x.shape[0]
    t = m // nseq
    ml, mo, cv, qa, ka, vb, skt, svt, g = _proj(x, w_in, b_in, wt, bt, ROW_TILE, nseq, True)
    c0 = jnp.zeros((nseq, ML_HEADS, LANES, LANES), F32)
    m0 = jnp.zeros((nseq, 1, LANES), F32)
    hml, c_ext, m_new = _mlstm(ml, mo, g, c0, m0, nw, nseq, MLSTM_CHUNK, MLSTM_CHUNK)
    hsb = _sb_prompt(qa, ka, vb, nseq, ATT_TQ, ATT_TK)
    x1, ptail = _outproj(hml, cv, hsb, x, w_out, cw, g1, b1, ROW_TILE, t)
    x2, gtail = _ffn(x1, w_up, fcw, w_down, g2, b2, ROW_TILE, t)
    last = lambda tail: tail.reshape(nseq, t // ROW_TILE, SUBLANES, -1)[:, -1, SUBLANES - 2:, :]
    states = (skt, svt,
              c_ext[:, :, :HEAD_DIM, :HEAD_DIM], c_ext[:, :, :HEAD_DIM, HEAD_DIM], m_new[:, 0, :ML_HEADS],
              last(ptail), last(gtail))
    return x2, states


def _layer_sample(x, nseq, wts, layer, cache_kt, cache_vt, page_table, c_prev, n_prev, m_prev, conv_prev,
                  ffn_prev):
    (w_in, b_in, wt, bt, nw, cw, sbias, w_out, g1, b1, w_up, fcw, w_down, g2, b2) = wts
    m = x.shape[0]
    t = m // nseq
    ml, mo, cv, qa, _, _, sk, sv, g = _proj(x, w_in, b_in, wt, bt, m, nseq, False)

    pad = lambda a: jnp.pad(a.reshape(nseq, t, -1), ((0, 0), (0, SAMPLE_PAD - t), (0, 0))).reshape(
        nseq * SAMPLE_PAD, -1)
    m0 = jnp.pad(m_prev, ((0, 0), (0, LANES - ML_HEADS)))[:, None, :]
    hml, c_ext, m_new = _mlstm(pad(ml), pad(mo), pad(g), _ext_state(c_prev, n_prev), m0, nw, nseq, SAMPLE_PAD, t)
    hml = hml.reshape(nseq, SAMPLE_PAD, ML_WIDTH)[:, :t].reshape(m, ML_WIDTH)

    q3 = qa.reshape(nseq, t, SB_HEADS, LANES)[..., :HEAD_DIM]
    eye = jnp.eye(SB_HEADS, dtype=BF16)
    qbd = jnp.einsum("bthd,hg->bhtgd", q3, eye).reshape(nseq, SB_HEADS * t, SB_WIDTH)
    bias = jnp.broadcast_to(jnp.repeat(sbias * LOG2E, t)[:, None], (SB_HEADS * t, PAGE_SIZE))
    new_t = lambda a: jnp.pad(jnp.swapaxes(a.reshape(nseq, t, SB_WIDTH), 1, 2),
                              ((0, 0), (0, 0), (0, PAGE_SIZE - t)))
    hsb = _sb_sample(qbd, bias, new_t(sk), new_t(sv), cache_kt, cache_vt, page_table, layer, t).reshape(
        m, SB_WIDTH)

    x1, p_all = _outproj(hml, cv, hsb, x, w_out, cw, g1, b1, m, t, _halo(conv_prev, t))
    x2, g_all = _ffn(x1, w_up, fcw, w_down, g2, b2, m, t, _halo(ffn_prev, t))
    last = lambda a: a.reshape(nseq, t, -1)[:, t - 2:, :]
    states = (sk.reshape(nseq, t, SB_HEADS, HEAD_DIM), sv.reshape(nseq, t, SB_HEADS, HEAD_DIM),
              c_ext[:, :, :HEAD_DIM, :HEAD_DIM], c_ext[:, :, :HEAD_DIM, HEAD_DIM], m_new[:, 0, :ML_HEADS],
              last(p_all), last(g_all))
    return x2, states


def _feature_major_pages(cache):
    d, p = cache.shape[:2]
    return jnp.transpose(cache, (0, 1, 3, 4, 2)).reshape(d, p, SB_WIDTH, PAGE_SIZE)


def _token_major_state(kt):
    d, b, _, t = kt.shape
    return jnp.transpose(kt.reshape(d, b, SB_HEADS, HEAD_DIM, t), (0, 1, 4, 2, 3))


def kernel(x_prompt, x_sample, cache_k, cache_v, state_mlstm_c, state_mlstm_n, state_mlstm_m, state_conv,
           state_ffn_conv, page_table, w_in, b_in, mlstm_norm_w, conv_w, sb_bias, w_out, ln1_g, ln1_b, ffn_w_up,
           ffn_conv_w, ffn_w_down, ln2_g, ln2_b):
    bp, tp, _ = x_prompt.shape
    bs, ts, _ = x_sample.shape
    depth = w_in.shape[0]
    w_in_p, b_in_p, wt_p, bt_p = _pack_w_in(w_in, b_in, sb_bias)
    w_out_b = w_out.astype(BF16)
    w_up_b = ffn_w_up.astype(BF16)
    w_down_b = ffn_w_down.astype(BF16)
    cw_p = _pad_rows8(conv_w)
    fcw_p = _pad_rows8(ffn_conv_w)
    ckt = _feature_major_pages(cache_k)
    cvt = _feature_major_pages(cache_v)

    yp = x_prompt.reshape(bp * tp, D_MODEL)
    ys = x_sample.reshape(bs * ts, D_MODEL)
    st_p = [[] for _ in range(7)]
    st_s = [[] for _ in range(7)]
    for l in range(depth):
        wts = (w_in_p[l], b_in_p[l], wt_p[l], bt_p[l], mlstm_norm_w[l][None, :], cw_p[l], sb_bias[l], w_out_b[l],
               ln1_g[l][None, :], ln1_b[l][None, :], w_up_b[l], fcw_p[l], w_down_b[l],
               ln2_g[l][None, :], ln2_b[l][None, :])
        yp, new_p = _layer_prompt(yp, bp, wts)
        ys, new_s = _layer_sample(ys, bs, wts, l, ckt, cvt, page_table, state_mlstm_c[l], state_mlstm_n[l],
                                  state_mlstm_m[l], state_conv[l], state_ffn_conv[l])
        for lst, a in zip(st_p, new_p):
            lst.append(a)
        for lst, a in zip(st_s, new_s):
            lst.append(a)
    outs_p = [jnp.stack(s) for s in st_p]
    outs_p[0] = _token_major_state(outs_p[0])
    outs_p[1] = _token_major_state(outs_p[1])
    outs_s = [jnp.stack(s) for s in st_s]
    return (yp.reshape(bp, tp, D_MODEL), ys.reshape(bs, ts, D_MODEL), *outs_p, *outs_s)
```

```python
import functools

import jax
import jax.numpy as jnp
from jax import lax
from jax.experimental import pallas as pl
from jax.experimental.pallas import tpu as pltpu

F32 = jnp.float32
BF16 = jnp.bfloat16

D_MODEL = 1024
DEPTH = 4
ML_HEADS = 4
HEAD_DIM = 64
ML_WIDTH = ML_HEADS * HEAD_DIM
CONV_DIM = 256
CONV_WIDTH = 3
SB_HEADS = 8
SB_WIDTH = SB_HEADS * HEAD_DIM
D_FF = 2816
PAGE_SIZE = 128
LN_EPS = 1e-5
DN_ALPHA = (2 * DEPTH) ** 0.25

LANES = 128
SUBLANES = 8
MIB = 1024 * 1024

C_ML = 0
ML_COLS = ML_HEADS * 3 * LANES
C_MO = C_ML + ML_COLS
C_CV = C_MO + ML_WIDTH
SB_AUG = SB_HEADS * LANES
C_QA = C_CV + 3 * CONV_DIM
C_KA = C_QA + SB_AUG
C_SK = C_KA + SB_AUG
C_SV = C_SK + SB_WIDTH
C_G = C_SV + SB_WIDTH
PROJ_COLS = C_G + LANES
LOG2E = 1.4426950408889634

MLSTM_CHUNK = 256
SAMPLE_PAD = 128
ROW_TILE = 512
ATT_TQ = 512
ATT_TK = 256
PAGES_PER_STEP = 16
FF_CHUNK = 256


def _cparams(sem, vmem_mib):
    return pltpu.CompilerParams(dimension_semantics=sem, vmem_limit_bytes=vmem_mib * MIB)


def _const_spec(shape):
    zeros = (0,) * len(shape)
    return pl.BlockSpec(shape, lambda *_: zeros)


def _softplus(z):
    return jnp.maximum(z, 0.0) + jnp.log(1.0 + jnp.exp(-jnp.abs(z)))


def _softplus2(z):
    sign_bit = jnp.uint32(0x80000000)
    neg_abs = lax.bitcast_convert_type(lax.bitcast_convert_type(z, jnp.uint32) | sign_bit, F32)
    return jnp.maximum(z, 0.0) + jnp.log2(1.0 + jnp.exp2(neg_abs))


def _layer_norm(y, g, b):
    mu = jnp.mean(y, axis=-1, keepdims=True)
    d = y - mu
    var = jnp.mean(d * d, axis=-1, keepdims=True)
    return d * lax.rsqrt(var + LN_EPS) * g + b


def _proj_kernel(x_ref, w_ref, b_ref, wt_ref, bt_ref, ml_ref, mo_ref, cv_ref, qa_ref, ka_ref, vb_ref, k_ref, v_ref,
                 g_ref, *, kv_transposed):
    x = x_ref[...].astype(BF16)

    def mm(c0, c1):
        return jnp.dot(x, w_ref[:, c0:c1], preferred_element_type=F32) + b_ref[:, c0:c1]

    for c in range(0, ML_COLS, 512):
        ml_ref[:, c:c + 512] = mm(C_ML + c, C_ML + c + 512).astype(BF16)
    mo_ref[...] = mm(C_MO, C_MO + ML_WIDTH)
    for c in range(0, 3 * CONV_DIM, CONV_DIM):
        cv_ref[:, c:c + CONV_DIM] = mm(C_CV + c, C_CV + c + CONV_DIM)
    for c in range(0, SB_AUG, 512):
        qa_ref[:, c:c + 512] = mm(C_QA + c, C_QA + c + 512).astype(BF16)
        ka_ref[:, c:c + 512] = mm(C_KA + c, C_KA + c + 512).astype(BF16)
    v = mm(C_SV, C_SV + SB_WIDTH)
    vb_ref[...] = v.astype(BF16)
    if kv_transposed:
        for r, ref in ((0, k_ref), (SB_WIDTH, v_ref)):
            ref[...] = lax.dot_general(wt_ref[r:r + SB_WIDTH, :], x, (((1,), (1,)), ((), ())),
                                       preferred_element_type=F32) + bt_ref[r:r + SB_WIDTH, :]
    else:
        k_ref[...] = mm(C_SK, C_SK + SB_WIDTH)
        v_ref[...] = v
    g_ref[...] = mm(C_G, C_G + LANES)


def _proj(x, w, b, wt, bt, tm, nseq, kv_transposed):
    m = x.shape[0]
    t = m // nseq
    row = lambda width: pl.BlockSpec((tm, width), lambda i: (i, 0))
    outs = [(ML_COLS, BF16), (ML_WIDTH, F32), (3 * CONV_DIM, F32), (SB_AUG, BF16), (SB_AUG, BF16),
            (SB_WIDTH, BF16), (SB_WIDTH, F32), (SB_WIDTH, F32), (LANES, F32)]
    out_specs = [row(wd) for wd, _ in outs]
    out_shape = [jax.ShapeDtypeStruct((m, wd), dt) for wd, dt in outs]
    if kv_transposed:
        tiles = t // tm
        for idx in (6, 7):
            out_specs[idx] = pl.BlockSpec((None, SB_WIDTH, tm), lambda i: (i // tiles, 0, i % tiles))
            out_shape[idx] = jax.ShapeDtypeStruct((nseq, SB_WIDTH, t), F32)
    return pl.pallas_call(
        functools.partial(_proj_kernel, kv_transposed=kv_transposed),
        grid=(m // tm,),
        in_specs=[row(D_MODEL),
                  pl.BlockSpec((D_MODEL, PROJ_COLS), lambda i: (0, 0), pipeline_mode=pl.Buffered(1)),
                  _const_spec((1, PROJ_COLS)),
                  _const_spec((2 * SB_WIDTH, D_MODEL)), _const_spec((2 * SB_WIDTH, 1))],
        out_specs=out_specs,
        out_shape=out_shape,
        compiler_params=_cparams(("arbitrary",), 56),
        name="proj",
    )(x, w, b, wt, bt)


def _mlstm_kernel(ml_ref, mo_ref, g_ref, c0_ref, m0_ref, nw_ref, up_ref, h_ref, c_ref, m_ref, cst, mst,
                  *, chunk, valid_len):
    L = chunk

    @pl.when(pl.program_id(1) == 0)
    def _():
        cst[...] = c0_ref[0]
        mst[...] = m0_ref[0]

    gt = g_ref[...].T
    t_ig = gt[0:SUBLANES, :]
    t_f = gt[SUBLANES:2 * SUBLANES, :]
    lf = -_softplus(-t_f)
    if valid_len < L:
        pos = lax.broadcasted_iota(jnp.int32, (SUBLANES, L), 1)
        t_ig = jnp.where(pos < valid_len, t_ig, -1e30)
        lf = jnp.where(pos < valid_len, lf, 0.0)
    hi = lf.astype(BF16)
    r1 = lf - hi.astype(F32)
    mid = r1.astype(BF16)
    lo = (r1 - mid.astype(F32)).astype(BF16)
    up = up_ref[...]
    b_row = (jnp.dot(hi, up, preferred_element_type=F32) + jnp.dot(mid, up, preferred_element_type=F32)
             + jnp.dot(lo, up, preferred_element_type=F32))
    a_row = t_ig - b_row
    cols = jnp.concatenate([a_row, b_row, jnp.zeros((LANES - 2 * SUBLANES, L), F32)], axis=0).T

    row_i = lax.broadcasted_iota(jnp.int32, (L, L), 0)
    col_i = lax.broadcasted_iota(jnp.int32, (L, L), 1)
    causal = row_i >= col_i
    lane_m = lax.broadcasted_iota(jnp.int32, (1, LANES), 1)
    m_all = mst[...]
    m_next = m_all

    heads = range(ML_HEADS)
    qs = [ml_ref[:, 3 * LANES * h:3 * LANES * h + LANES] for h in heads]
    ks = [ml_ref[:, 3 * LANES * h + LANES:3 * LANES * h + 2 * LANES] for h in heads]
    vs = [ml_ref[:, 3 * LANES * h + 2 * LANES:3 * LANES * h + 3 * LANES] for h in heads]
    c_old = [cst[h] for h in heads]
    s_all = [lax.dot_general(qs[h], ks[h], (((1,), (1,)), ((), ())), preferred_element_type=F32) for h in heads]
    qc_all = [jnp.dot(qs[h], c_old[h].astype(BF16), preferred_element_type=F32) for h in heads]

    w_all, kw_all, inter_all, mt_all, decay_all = [], [], [], [], []
    for h in heads:
        a_c = cols[:, h:h + 1]
        b_c = cols[:, SUBLANES + h:SUBLANES + h + 1]
        a_r = a_row[h:h + 1, :]
        m_prev = m_all[:, h:h + 1]
        log_d = jnp.where(causal, b_c + a_r, -jnp.inf)
        m_loc = jnp.max(log_d, axis=1, keepdims=True)
        m_inter = b_c + m_prev
        m_t = jnp.maximum(m_inter, m_loc)
        d = jnp.exp(log_d - m_t)
        inter_all.append(jnp.exp(m_inter - m_t))
        mt_all.append(m_t)
        w_all.append((s_all[h] * d).astype(BF16))
        m_new = m_t[L - 1:L, :]
        b_last = b_c[L - 1:L, :]
        w_s = jnp.exp(b_last + a_c - m_new)
        decay_all.append(jnp.exp(b_last + m_prev - m_new))
        kw_all.append((ks[h].astype(F32) * w_s).astype(BF16))
        m_next = jnp.where(lane_m == h, m_new, m_next)

    wv_all = [jnp.dot(w_all[h], vs[h], preferred_element_type=F32) for h in heads]
    upd_all = [lax.dot_general(kw_all[h], vs[h], (((0,), (0,)), ((), ())), preferred_element_type=F32)
               for h in heads]

    for h in heads:
        nd = wv_all[h] + inter_all[h] * qc_all[h]
        num = nd[:, 0:HEAD_DIM]
        den = nd[:, HEAD_DIM:HEAD_DIM + 1]
        hh = num / jnp.maximum(jnp.abs(den), jnp.exp(-mt_all[h]))
        mu = jnp.mean(hh, axis=-1, keepdims=True)
        dh = hh - mu
        var = jnp.mean(dh * dh, axis=-1, keepdims=True)
        o = mo_ref[:, HEAD_DIM * h:HEAD_DIM * (h + 1)]
        gate = 1.0 / (1.0 + jnp.exp(-o))
        hn = dh * lax.rsqrt(var + LN_EPS) * nw_ref[:, HEAD_DIM * h:HEAD_DIM * (h + 1)] * gate
        h_ref[:, HEAD_DIM * h:HEAD_DIM * (h + 1)] = hn.astype(BF16)
        cst[h] = decay_all[h] * c_old[h] + upd_all[h]

    mst[...] = m_next
    c_ref[0] = cst[...]
    m_ref[0] = m_next


def _mlstm(ml, mo, g, c0, m0, nw, nseq, chunk, valid_len):
    m = ml.shape[0]
    nc = m // nseq // chunk
    up = (lax.broadcasted_iota(jnp.int32, (chunk, chunk), 0)
          <= lax.broadcasted_iota(jnp.int32, (chunk, chunk), 1)).astype(BF16)
    row = lambda width: pl.BlockSpec((chunk, width), lambda b, j: (b * nc + j, 0))
    st_c = pl.BlockSpec((1, ML_HEADS, LANES, LANES), lambda b, j: (b, 0, 0, 0))
    st_m = pl.BlockSpec((1, 1, LANES), lambda b, j: (b, 0, 0))
    return pl.pallas_call(
        functools.partial(_mlstm_kernel, chunk=chunk, valid_len=valid_len),
        grid=(nseq, nc),
        in_specs=[row(ML_COLS), row(ML_WIDTH), row(LANES), st_c, st_m, _const_spec((1, ML_WIDTH)),
                  _const_spec((chunk, chunk))],
        out_specs=[row(ML_WIDTH), st_c, st_m],
        out_shape=[jax.ShapeDtypeStruct((m, ML_WIDTH), BF16),
                   jax.ShapeDtypeStruct((nseq, ML_HEADS, LANES, LANES), F32),
                   jax.ShapeDtypeStruct((nseq, 1, LANES), F32)],
        scratch_shapes=[pltpu.VMEM((ML_HEADS, LANES, LANES), F32), pltpu.VMEM((1, LANES), F32)],
        compiler_params=_cparams(("arbitrary", "arbitrary"), 32),
        name="mlstm",
    )(ml, mo, g, c0, m0, nw, up)


def _sb_prompt_kernel(q_ref, k_ref, v_ref, tri_ref, o_ref, acc_ref, carry_ref, *, tq, tk):
    i = pl.program_id(2)
    nkb = tq // tk
    tri = tri_ref[...]
    acc_ref[...] = jnp.zeros_like(acc_ref)
    carry_ref[...] = jnp.zeros_like(carry_ref)

    def steps(blocks, masked):
        chains = [(n, hh) for n in range(len(blocks)) for hh in range(2)]
        starts = [pl.multiple_of(j * tk, tk) for j in blocks]
        keep = []
        if masked:
            for j in blocks:
                qpos = i * tq + lax.broadcasted_iota(jnp.int32, (tq, tk), 0)
                kpos = j * tk + lax.broadcasted_iota(jnp.int32, (tq, tk), 1)
                keep.append(kpos < qpos)
        z = [lax.dot_general(q_ref[:, hh * LANES:(hh + 1) * LANES],
                             k_ref[pl.ds(starts[j], tk), hh * LANES:(hh + 1) * LANES],
                             (((1,), (1,)), ((), ())), preferred_element_type=F32) for j, hh in chains]
        cum = []
        for c, (j, hh) in enumerate(chains):
            sp = _softplus2(z[c])
            if masked:
                sp = jnp.where(keep[j], sp, 0.0)
            cum.append(jnp.dot(sp.astype(BF16), tri, preferred_element_type=F32))
        pv = []
        for c, (j, hh) in enumerate(chains):
            a = jnp.exp2(z[c] + cum[c])
            if masked:
                a = jnp.where(keep[j], a, 0.0)
            pv.append(jnp.dot(a.astype(BF16), v_ref[pl.ds(starts[j], tk), :], preferred_element_type=F32))
        for c, (j, hh) in enumerate(chains):
            carry = carry_ref[hh]
            acc_ref[hh] += jnp.exp2(carry) * pv[c]
            carry_ref[hh] = carry + cum[c][:, 0:1]

    steps([i * nkb + dblk for dblk in reversed(range(nkb))], True)

    assert nkb == 2
    top = i * nkb - 1
    odd = i % 2

    @pl.when(odd == 1)
    def _():
        steps([top, top - 1], False)

    def body(t, _):
        j0 = top - 2 * odd - 4 * t
        steps([j0, j0 - 1, j0 - 2, j0 - 3], False)
        return 0

    lax.fori_loop(0, i // 2, body, 0)

    lane = lax.broadcasted_iota(jnp.int32, (tq, LANES), 1)
    o_ref[...] = jnp.where(lane >= HEAD_DIM, acc_ref[1], acc_ref[0]).astype(BF16)


def _sb_prompt(qa, ka, vb, nseq, tq, tk):
    m = qa.shape[0]
    t = m // nseq
    nq = t // tq
    tri = -(lax.broadcasted_iota(jnp.int32, (tk, tk), 0)
            >= lax.broadcasted_iota(jnp.int32, (tk, tk), 1)).astype(BF16)
    return pl.pallas_call(
        functools.partial(_sb_prompt_kernel, tq=tq, tk=tk),
        grid=(nseq, SB_HEADS // 2, nq),
        in_specs=[pl.BlockSpec((tq, 2 * LANES), lambda b, hp, i: (b * nq + i, hp)),
                  pl.BlockSpec((t, 2 * LANES), lambda b, hp, i: (b, hp)),
                  pl.BlockSpec((t, LANES), lambda b, hp, i: (b, hp)),
                  pl.BlockSpec((tk, tk), lambda b, hp, i: (0, 0))],
        out_specs=pl.BlockSpec((tq, LANES), lambda b, hp, i: (b * nq + i, hp)),
        out_shape=jax.ShapeDtypeStruct((m, SB_WIDTH), BF16),
        scratch_shapes=[pltpu.VMEM((2, tq, LANES), F32), pltpu.VMEM((2, tq, 1), F32)],
        compiler_params=_cparams(("arbitrary", "arbitrary", "arbitrary"), 40),
        name="sb_prompt",
    )(qa, ka, vb, tri)


def _sb_sample_kernel(pt_ref, qbd_ref, bias_ref, kn_ref, vn_ref, tri_ref, *refs, pps, nq):
    k_refs = refs[0:pps]
    v_refs = refs[pps:2 * pps]
    o_ref = refs[2 * pps]
    acc_ref, carry_ref = refs[2 * pps + 1:]
    j = pl.program_id(1)
    qbd = qbd_ref[0]
    bias = bias_ref[...]
    tri = tri_ref[...]

    def blocks(kts, vts, keep):
        z = [jnp.dot(qbd, kt[...].astype(BF16), preferred_element_type=F32) + bias for kt in kts]
        cum = []
        for zc in z:
            sp = _softplus2(zc)
            if keep is not None:
                sp = jnp.where(keep, sp, 0.0)
            cum.append(jnp.dot(sp.astype(BF16), tri, preferred_element_type=F32))
        pv = []
        for zc, cc, vt in zip(z, cum, vts):
            a = jnp.exp2(zc + cc)
            if keep is not None:
                a = jnp.where(keep, a, 0.0)
            pv.append(lax.dot_general(a.astype(BF16), vt[...].astype(BF16), (((1,), (1,)), ((), ())),
                                      preferred_element_type=F32))
        acc = acc_ref[...]
        carry = carry_ref[...]
        for pc, cc in zip(pv, cum):
            acc = acc + jnp.exp2(carry) * pc
            carry = carry + cc[:, 0:1]
        acc_ref[...] = acc
        carry_ref[...] = carry

    @pl.when(j == 0)
    def _():
        acc_ref[...] = jnp.zeros_like(acc_ref)
        carry_ref[...] = jnp.zeros_like(carry_ref)
        t_i = lax.broadcasted_iota(jnp.int32, (SB_HEADS * nq, PAGE_SIZE), 0) % nq
        s_i = lax.broadcasted_iota(jnp.int32, (SB_HEADS * nq, PAGE_SIZE), 1)
        blocks([kn_ref.at[0]], [vn_ref.at[0]], s_i < t_i)

    blocks(k_refs, v_refs, None)

    @pl.when(j == pl.num_programs(1) - 1)
    def _():
        acc = acc_ref[...]
        lane_h = lax.broadcasted_iota(jnp.int32, (nq, SB_WIDTH), 1) // HEAD_DIM
        res = jnp.zeros((nq, SB_WIDTH), F32)
        for h in range(SB_HEADS):
            res = jnp.where(lane_h == h, acc[nq * h:nq * (h + 1), :], res)
        o_ref[0] = res


def _sb_sample(qbd, bias, kt_new, vt_new, cache_kt, cache_vt, page_table, layer, nq):
    bsz, n_pages = page_table.shape
    pps = PAGES_PER_STEP
    steps = n_pages // pps
    rows = SB_HEADS * nq
    tri = -(lax.broadcasted_iota(jnp.int32, (PAGE_SIZE, PAGE_SIZE), 0)
            >= lax.broadcasted_iota(jnp.int32, (PAGE_SIZE, PAGE_SIZE), 1)).astype(BF16)

    def page_spec(p):
        def imap(b, j, pt):
            return (layer, pt[b * n_pages + (n_pages - 1 - (j * pps + p))], 0, 0)
        return pl.BlockSpec((None, None, SB_WIDTH, PAGE_SIZE), imap)

    per_seq = lambda shape: pl.BlockSpec((1,) + shape, lambda b, j, pt: (b, 0, 0))
    grid_spec = pltpu.PrefetchScalarGridSpec(
        num_scalar_prefetch=1,
        grid=(bsz, steps),
        in_specs=[per_seq((rows, SB_WIDTH)), pl.BlockSpec((rows, PAGE_SIZE), lambda b, j, pt: (0, 0)),
                  per_seq((SB_WIDTH, PAGE_SIZE)), per_seq((SB_WIDTH, PAGE_SIZE)),
                  pl.BlockSpec((PAGE_SIZE, PAGE_SIZE), lambda b, j, pt: (0, 0))]
                 + [page_spec(p) for p in range(pps)] + [page_spec(p) for p in range(pps)],
        out_specs=per_seq((nq, SB_WIDTH)),
        scratch_shapes=[pltpu.VMEM((rows, SB_WIDTH), F32), pltpu.VMEM((rows, 1), F32)],
    )
    return pl.pallas_call(
        functools.partial(_sb_sample_kernel, pps=pps, nq=nq),
        grid_spec=grid_spec,
        out_shape=jax.ShapeDtypeStruct((bsz, nq, SB_WIDTH), F32),
        compiler_params=_cparams(("arbitrary", "arbitrary"), 40),
        name="sb_sample",
    )(page_table.reshape(-1), qbd, bias, kt_new, vt_new, tri, *([cache_kt] * pps), *([cache_vt] * pps))


def _conv3(p, s_ref, cs, w_ref, tm, halo):
    s_ref[SUBLANES:SUBLANES + tm, cs] = p
    p1 = s_ref[SUBLANES - 1:SUBLANES - 1 + tm, cs]
    p2 = s_ref[SUBLANES - 2:SUBLANES - 2 + tm, cs]
    if halo is not None:
        e1, e2, seq_len = halo
        tpos = lax.broadcasted_iota(jnp.int32, p.shape, 0) % seq_len
        p1 = jnp.where(tpos == 0, e1, p1)
        p2 = jnp.where(tpos < 2, e2, p2)
    return w_ref[0:1, cs] * p2 + w_ref[1:2, cs] * p1 + w_ref[2:3, cs] * p


def _start_tile(s_ref, tiles_per_seq):
    @pl.when(pl.program_id(0) % tiles_per_seq == 0)
    def _():
        s_ref[0:SUBLANES, :] = jnp.zeros((SUBLANES, s_ref.shape[1]), F32)


def _carry_rows(s_ref, tm):
    s_ref[0:SUBLANES, :] = s_ref[tm:tm + SUBLANES, :]


def _outproj_kernel(*refs, tm, tiles_per_seq, seq_len):
    sample = seq_len < tm
    if sample:
        hml_ref, cv_ref, hsb_ref, x_ref, w_ref, cw_ref, g_ref, b_ref, e1_ref, e2_ref, y_ref, tail_ref, s_ref = refs
        halo = (e1_ref[...], e2_ref[...], seq_len)
    else:
        hml_ref, cv_ref, hsb_ref, x_ref, w_ref, cw_ref, g_ref, b_ref, y_ref, tail_ref, s_ref = refs
        halo = None
    _start_tile(s_ref, tiles_per_seq)
    cb = cv_ref[:, 0:CONV_DIM]
    p = cv_ref[:, CONV_DIM:2 * CONV_DIM] * cv_ref[:, 2 * CONV_DIM:3 * CONV_DIM]
    u = _conv3(p, s_ref, slice(None), cw_ref, tm, halo)
    if sample:
        tail_ref[...] = p
    else:
        tail_ref[...] = p[tm - SUBLANES:tm, :]
        _carry_rows(s_ref, tm)
    h_conv = (cb * u).astype(BF16)
    mix = (jnp.dot(hml_ref[...], w_ref[0:ML_WIDTH, :], preferred_element_type=F32)
           + jnp.dot(h_conv, w_ref[ML_WIDTH:ML_WIDTH + CONV_DIM, :], preferred_element_type=F32)
           + jnp.dot(hsb_ref[...].astype(BF16), w_ref[ML_WIDTH + CONV_DIM:, :], preferred_element_type=F32))
    y_ref[...] = _layer_norm(DN_ALPHA * x_ref[...] + mix, g_ref[...], b_ref[...])


def _outproj(hml, cv, hsb, x, w, cw, g, b, tm, seq_len, halo=None):
    m = x.shape[0]
    sample = seq_len < tm
    tiles_per_seq = max(seq_len // tm, 1)
    row = lambda width: pl.BlockSpec((tm, width), lambda i: (i, 0))
    tail_rows = tm if sample else SUBLANES
    in_specs = [row(ML_WIDTH), row(3 * CONV_DIM), row(SB_WIDTH), row(D_MODEL),
                _const_spec((D_MODEL, D_MODEL)), _const_spec((SUBLANES, CONV_DIM)),
                _const_spec((1, D_MODEL)), _const_spec((1, D_MODEL))]
    args = [hml, cv, hsb, x, w, cw, g, b]
    if sample:
        in_specs += [row(CONV_DIM), row(CONV_DIM)]
        args += list(halo)
    return pl.pallas_call(
        functools.partial(_outproj_kernel, tm=tm, tiles_per_seq=tiles_per_seq, seq_len=seq_len),
        grid=(m // tm,),
        in_specs=in_specs,
        out_specs=[row(D_MODEL), pl.BlockSpec((tail_rows, CONV_DIM), lambda i: (i, 0))],
        out_shape=[jax.ShapeDtypeStruct((m, D_MODEL), F32),
                   jax.ShapeDtypeStruct((m // tm * tail_rows, CONV_DIM), F32)],
        scratch_shapes=[pltpu.VMEM((tm + SUBLANES, CONV_DIM), F32)],
        compiler_params=_cparams(("arbitrary",), 40),
        name="outproj",
    )(*args)


def _ffn_kernel(*refs, tm, tiles_per_seq, seq_len):
    sample = seq_len < tm
    if sample:
        x_ref, wu_ref, cw_ref, wd_ref, g_ref, b_ref, e1_ref, e2_ref, y_ref, tail_ref, s_ref, acc_ref = refs
    else:
        x_ref, wu_ref, cw_ref, wd_ref, g_ref, b_ref, y_ref, tail_ref, s_ref, acc_ref = refs
    _start_tile(s_ref, tiles_per_seq)
    x = x_ref[...]
    xb = x.astype(BF16)
    n_chunks = D_FF // FF_CHUNK

    def up(c):
        return (jnp.dot(xb, wu_ref[:, c * FF_CHUNK:(c + 1) * FF_CHUNK], preferred_element_type=F32),
                jnp.dot(xb, wu_ref[:, D_FF + c * FF_CHUNK:D_FF + (c + 1) * FF_CHUNK], preferred_element_type=F32))

    nxt = up(0)
    for c in range(n_chunks):
        cs = slice(c * FF_CHUNK, (c + 1) * FF_CHUNK)
        g_pre, val = nxt
        if c + 1 < n_chunks:
            nxt = up(c + 1)
        halo = (e1_ref[:, cs], e2_ref[:, cs], seq_len) if sample else None
        g_conv = _conv3(g_pre, s_ref, cs, cw_ref, tm, halo)
        hid = (g_conv / (1.0 + jnp.exp(-g_conv)) * val).astype(BF16)
        part = jnp.dot(hid, wd_ref[cs, :], preferred_element_type=F32)
        if c == 0:
            acc_ref[...] = part
        else:
            acc_ref[...] += part
    if sample:
        tail_ref[...] = s_ref[SUBLANES:SUBLANES + tm, :]
    else:
        tail_ref[...] = s_ref[tm:tm + SUBLANES, :]
        _carry_rows(s_ref, tm)
    y_ref[...] = _layer_norm(DN_ALPHA * x + acc_ref[...], g_ref[...], b_ref[...])


def _ffn(x, wu, cw, wd, g, b, tm, seq_len, halo=None):
    m = x.shape[0]
    sample = seq_len < tm
    tiles_per_seq = max(seq_len // tm, 1)
    row = lambda width: pl.BlockSpec((tm, width), lambda i: (i, 0))
    tail_rows = tm if sample else SUBLANES
    in_specs = [row(D_MODEL),
                pl.BlockSpec((D_MODEL, 2 * D_FF), lambda i: (0, 0), pipeline_mode=pl.Buffered(1)),
                _const_spec((SUBLANES, D_FF)),
                pl.BlockSpec((D_FF, D_MODEL), lambda i: (0, 0), pipeline_mode=pl.Buffered(1)),
                _const_spec((1, D_MODEL)), _const_spec((1, D_MODEL))]
    args = [x, wu, cw, wd, g, b]
    if sample:
        in_specs += [row(D_FF), row(D_FF)]
        args += list(halo)
    return pl.pallas_call(
        functools.partial(_ffn_kernel, tm=tm, tiles_per_seq=tiles_per_seq, seq_len=seq_len),
        grid=(m // tm,),
        in_specs=in_specs,
        out_specs=[row(D_MODEL), pl.BlockSpec((tail_rows, D_FF), lambda i: (i, 0))],
        out_shape=[jax.ShapeDtypeStruct((m, D_MODEL), F32),
                   jax.ShapeDtypeStruct((m // tm * tail_rows, D_FF), F32)],
        scratch_shapes=[pltpu.VMEM((tm + SUBLANES, D_FF), F32), pltpu.VMEM((tm, D_MODEL), F32)],
        compiler_params=_cparams(("arbitrary",), 56),
        name="ffn",
    )(*args)


def _split3(x):
    hi = x.astype(BF16).astype(F32)
    mid = (x - hi).astype(BF16).astype(F32)
    lo = (x - hi - mid).astype(BF16).astype(F32)
    return hi, mid, lo


def _pack_w_in(w_in, b_in, sb_bias):
    depth = w_in.shape[0]
    scale = HEAD_DIM ** -0.5
    b2 = _split3(sb_bias * LOG2E)
    zw = lambda n: jnp.zeros((depth, D_MODEL, n), F32)
    zb = lambda n: jnp.zeros((depth, n), F32)
    wp, bp = [], []
    for h in range(ML_HEADS):
        lo, hi = HEAD_DIM * h, HEAD_DIM * (h + 1)
        wp += [w_in[:, :, lo:hi], zw(HEAD_DIM)]
        bp += [b_in[:, lo:hi], zb(HEAD_DIM)]
        wp += [w_in[:, :, ML_WIDTH + lo:ML_WIDTH + hi] * scale, zw(HEAD_DIM)]
        bp += [b_in[:, ML_WIDTH + lo:ML_WIDTH + hi] * scale, zb(HEAD_DIM)]
        wp += [w_in[:, :, 2 * ML_WIDTH + lo:2 * ML_WIDTH + hi], zw(HEAD_DIM)]
        bp += [b_in[:, 2 * ML_WIDTH + lo:2 * ML_WIDTH + hi], jnp.ones((depth, 1), F32), zb(HEAD_DIM - 1)]
    gi = 4 * ML_WIDTH
    cv0 = gi + 2 * ML_HEADS
    sq0 = cv0 + 3 * CONV_DIM
    wp += [w_in[:, :, 3 * ML_WIDTH:4 * ML_WIDTH], w_in[:, :, cv0:sq0]]
    bp += [b_in[:, 3 * ML_WIDTH:4 * ML_WIDTH], b_in[:, cv0:sq0]]
    sk0 = sq0 + SB_WIDTH
    for h in range(SB_HEADS):
        lo, hi = sq0 + HEAD_DIM * h, sq0 + HEAD_DIM * (h + 1)
        wp += [w_in[:, :, lo:hi] * (scale * LOG2E), zw(HEAD_DIM)]
        bp += [b_in[:, lo:hi] * (scale * LOG2E)] + [part[:, h:h + 1] for part in b2] + [zb(HEAD_DIM - 3)]
    for h in range(SB_HEADS):
        lo, hi = sk0 + HEAD_DIM * h, sk0 + HEAD_DIM * (h + 1)
        wp += [w_in[:, :, lo:hi], zw(HEAD_DIM)]
        bp += [b_in[:, lo:hi], jnp.ones((depth, 3), F32), zb(HEAD_DIM - 3)]
    wp += [w_in[:, :, sk0:],
           w_in[:, :, gi:gi + ML_HEADS], zw(SUBLANES - ML_HEADS),
           w_in[:, :, gi + ML_HEADS:gi + 2 * ML_HEADS], zw(LANES - SUBLANES - ML_HEADS)]
    bp += [b_in[:, sk0:],
           b_in[:, gi:gi + ML_HEADS], zb(SUBLANES - ML_HEADS),
           b_in[:, gi + ML_HEADS:gi + 2 * ML_HEADS], zb(LANES - SUBLANES - ML_HEADS)]
    w = jnp.concatenate(wp, axis=-1).astype(BF16)
    b = jnp.concatenate(bp, axis=-1)[:, None, :]
    assert w.shape[-1] == PROJ_COLS and b.shape[-1] == PROJ_COLS
    wt = jnp.swapaxes(w_in[:, :, sk0:], 1, 2).astype(BF16)
    bt = b_in[:, sk0:, None]
    return w, b, wt, bt


def _pad_rows8(a):
    return jnp.pad(a, ((0, 0), (0, SUBLANES - a.shape[1]), (0, 0)))


def _ext_state(c, n):
    ext = jnp.concatenate([c, n[..., None]], axis=-1)
    return jnp.pad(ext, ((0, 0), (0, 0), (0, LANES - HEAD_DIM), (0, LANES - HEAD_DIM - 1)))


def _halo(prev, seq_len):
    bsz, _, c = prev.shape
    z = jnp.zeros((bsz, seq_len, c), F32)
    e1 = z.at[:, 0].set(prev[:, 1])
    e2 = z.at[:, 0].set(prev[:, 0]).at[:, 1].set(prev[:, 1])
    return e1.reshape(bsz * seq_len, c), e2.reshape(bsz * seq_len, c)


def _layer_prompt(x, nseq, wts):
    (w_in, b_in, wt, bt, nw, cw, sbias, w_out, g1, b1, w_up, fcw, w_down, g2, b2) = wts
    m = x.shape[0]
    t = m // nseq
    ml, mo, cv, qa, ka, vb, skt, svt, g = _proj(x, w_in, b_in, wt, bt, ROW_TILE, nseq, True)
    c0 = jnp.zeros((nseq, ML_HEADS, LANES, LANES), F32)
    m0 = jnp.zeros((nseq, 1, LANES), F32)
    hml, c_ext, m_new = _mlstm(ml, mo, g, c0, m0, nw, nseq, MLSTM_CHUNK, MLSTM_CHUNK)
    hsb = _sb_prompt(qa, ka, vb, nseq, ATT_TQ, ATT_TK)
    x1, ptail = _outproj(hml, cv, hsb, x, w_out, cw, g1, b1, ROW_TILE, t)
    x2, gtail = _ffn(x1, w_up, fcw, w_down, g2, b2, ROW_TILE, t)
    last = lambda tail: tail.reshape(nseq, t // ROW_TILE, SUBLANES, -1)[:, -1, SUBLANES - 2:, :]
    states = (skt, svt,
              c_ext[:, :, :HEAD_DIM, :HEAD_DIM], c_ext[:, :, :HEAD_DIM, HEAD_DIM], m_new[:, 0, :ML_HEADS],
              last(ptail), last(gtail))
    return x2, states


def _layer_sample(x, nseq, wts, layer, cache_kt, cache_vt, page_table, c_prev, n_prev, m_prev, conv_prev,
                  ffn_prev):
    (w_in, b_in, wt, bt, nw, cw, sbias, w_out, g1, b1, w_up, fcw, w_down, g2, b2) = wts
    m = x.shape[0]
    t = m // nseq
    ml, mo, cv, qa, _, _, sk, sv, g = _proj(x, w_in, b_in, wt, bt, m, nseq, False)

    pad = lambda a: jnp.pad(a.reshape(nseq, t, -1), ((0, 0), (0, SAMPLE_PAD - t), (0, 0))).reshape(
        nseq * SAMPLE_PAD, -1)
    m0 = jnp.pad(m_prev, ((0, 0), (0, LANES - ML_HEADS)))[:, None, :]
    hml, c_ext, m_new = _mlstm(pad(ml), pad(mo), pad(g), _ext_state(c_prev, n_prev), m0, nw, nseq, SAMPLE_PAD, t)
    hml = hml.reshape(nseq, SAMPLE_PAD, ML_WIDTH)[:, :t].reshape(m, ML_WIDTH)

    q3 = qa.reshape(nseq, t, SB_HEADS, LANES)[..., :HEAD_DIM]
    eye = jnp.eye(SB_HEADS, dtype=BF16)
    qbd = jnp.einsum("bthd,hg->bhtgd", q3, eye).reshape(nseq, SB_HEADS * t, SB_WIDTH)
    bias = jnp.broadcast_to(jnp.repeat(sbias * LOG2E, t)[:, None], (SB_HEADS * t, PAGE_SIZE))
    new_t = lambda a: jnp.pad(jnp.swapaxes(a.reshape(nseq, t, SB_WIDTH), 1, 2),
                              ((0, 0), (0, 0), (0, PAGE_SIZE - t)))
    hsb = _sb_sample(qbd, bias, new_t(sk), new_t(sv), cache_kt, cache_vt, page_table, layer, t).reshape(
        m, SB_WIDTH)

    x1, p_all = _outproj(hml, cv, hsb, x, w_out, cw, g1, b1, m, t, _halo(conv_prev, t))
    x2, g_all = _ffn(x1, w_up, fcw, w_down, g2, b2, m, t, _halo(ffn_prev, t))
    last = lambda a: a.reshape(nseq, t, -1)[:, t - 2:, :]
    states = (sk.reshape(nseq, t, SB_HEADS, HEAD_DIM), sv.reshape(nseq, t, SB_HEADS, HEAD_DIM),
              c_ext[:, :, :HEAD_DIM, :HEAD_DIM], c_ext[:, :, :HEAD_DIM, HEAD_DIM], m_new[:, 0, :ML_HEADS],
              last(p_all), last(g_all))
    return x2, states


def _feature_major_pages(cache):
    d, p = cache.shape[:2]
    return jnp.transpose(cache, (0, 1, 3, 4, 2)).reshape(d, p, SB_WIDTH, PAGE_SIZE)


def _token_major_state(kt):
    d, b, _, t = kt.shape
    return jnp.transpose(kt.reshape(d, b, SB_HEADS, HEAD_DIM, t), (0, 1, 4, 2, 3))


def kernel(x_prompt, x_sample, cache_k, cache_v, state_mlstm_c, state_mlstm_n, state_mlstm_m, state_conv,
           state_ffn_conv, page_table, w_in, b_in, mlstm_norm_w, conv_w, sb_bias, w_out, ln1_g, ln1_b, ffn_w_up,
           ffn_conv_w, ffn_w_down, ln2_g, ln2_b):
    bp, tp, _ = x_prompt.shape
    bs, ts, _ = x_sample.shape
    depth = w_in.shape[0]
    w_in_p, b_in_p, wt_p, bt_p = _pack_w_in(w_in, b_in, sb_bias)
    w_out_b = w_out.astype(BF16)
    w_up_b = ffn_w_up.astype(BF16)
    w_down_b = ffn_w_down.astype(BF16)
    cw_p = _pad_rows8(conv_w)
    fcw_p = _pad_rows8(ffn_conv_w)
    ckt = _feature_major_pages(cache_k)
    cvt = _feature_major_pages(cache_v)

    yp = x_prompt.reshape(bp * tp, D_MODEL)
    ys = x_sample.reshape(bs * ts, D_MODEL)
    st_p = [[] for _ in range(7)]
    st_s = [[] for _ in range(7)]
    for l in range(depth):
        wts = (w_in_p[l], b_in_p[l], wt_p[l], bt_p[l], mlstm_norm_w[l][None, :], cw_p[l], sb_bias[l], w_out_b[l],
               ln1_g[l][None, :], ln1_b[l][None, :], w_up_b[l], fcw_p[l], w_down_b[l],
               ln2_g[l][None, :], ln2_b[l][None, :])
        yp, new_p = _layer_prompt(yp, bp, wts)
        ys, new_s = _layer_sample(ys, bs, wts, l, ckt, cvt, page_table, state_mlstm_c[l], state_mlstm_n[l],
                                  state_mlstm_m[l], state_conv[l], state_ffn_conv[l])
        for lst, a in zip(st_p, new_p):
            lst.append(a)
        for lst, a in zip(st_s, new_s):
            lst.append(a)
    outs_p = [jnp.stack(s) for s in st_p]
    outs_p[0] = _token_major_state(outs_p[0])
    outs_p[1] = _token_major_state(outs_p[1])
    outs_s = [jnp.stack(s) for s in st_s]
    return (yp.reshape(bp, tp, D_MODEL), ys.reshape(bs, ts, D_MODEL), *outs_p, *outs_s)
```

```python
import functools

import jax
import jax.numpy as jnp
from jax import lax
from jax.experimental import pallas as pl
from jax.experimental.pallas import tpu as pltpu

F32 = jnp.float32
BF16 = jnp.bfloat16

D_MODEL = 1024
DEPTH = 4
ML_HEADS = 4
HEAD_DIM = 64
ML_WIDTH = ML_HEADS * HEAD_DIM
CONV_DIM = 256
CONV_WIDTH = 3
SB_HEADS = 8
SB_WIDTH = SB_HEADS * HEAD_DIM
D_FF = 2816
PAGE_SIZE = 128
LN_EPS = 1e-5
DN_ALPHA = (2 * DEPTH) ** 0.25

LANES = 128
SUBLANES = 8
MIB = 1024 * 1024

C_ML = 0
ML_COLS = ML_HEADS * 3 * LANES
C_MO = C_ML + ML_COLS
C_CV = C_MO + ML_WIDTH
SB_AUG = SB_HEADS * LANES
C_QA = C_CV + 3 * CONV_DIM
C_KA = C_QA + SB_AUG
C_SK = C_KA + SB_AUG
C_SV = C_SK + SB_WIDTH
C_G = C_SV + SB_WIDTH
PROJ_COLS = C_G + LANES
LOG2E = 1.4426950408889634

MLSTM_CHUNK = 256
MLSTM_SEQS = 1
SAMPLE_PAD = 128
ROW_TILE = 512
ATT_TQ = 512
ATT_TK = 256
PAGES_PER_STEP = 16
FF_CHUNK = 256


def _cparams(sem, vmem_mib):
    return pltpu.CompilerParams(dimension_semantics=sem, vmem_limit_bytes=vmem_mib * MIB)


def _const_spec(shape):
    zeros = (0,) * len(shape)
    return pl.BlockSpec(shape, lambda *_: zeros)


def _layer_spec(shape, layer, **kwargs):
    index = (layer,) + (0,) * len(shape)
    return pl.BlockSpec((None,) + tuple(shape), lambda *_: index, **kwargs)


def _softplus(z):
    return jnp.maximum(z, 0.0) + jnp.log(1.0 + jnp.exp(-jnp.abs(z)))


def _softplus2(z):
    sign_bit = jnp.uint32(0x80000000)
    neg_abs = lax.bitcast_convert_type(lax.bitcast_convert_type(z, jnp.uint32) | sign_bit, F32)
    return jnp.maximum(z, 0.0) + jnp.log2(1.0 + jnp.exp2(neg_abs))


def _layer_norm(y, g, b):
    mu = jnp.mean(y, axis=-1, keepdims=True)
    d = y - mu
    var = jnp.mean(d * d, axis=-1, keepdims=True)
    return d * lax.rsqrt(var + LN_EPS) * g + b


def _proj_kernel(x_ref, w_ref, b_ref, wt_ref, bt_ref, ml_ref, mo_ref, cv_ref, qa_ref, ka_ref, vb_ref, k_ref, v_ref,
                 g_ref, *, kv_transposed):
    x = x_ref[...].astype(BF16)

    def mm(c0, c1):
        return jnp.dot(x, w_ref[:, c0:c1], preferred_element_type=F32) + b_ref[:, c0:c1]

    for c in range(0, ML_COLS, 512):
        ml_ref[:, c:c + 512] = mm(C_ML + c, C_ML + c + 512).astype(BF16)
    mo_ref[...] = mm(C_MO, C_MO + ML_WIDTH)
    for c in range(0, 3 * CONV_DIM, CONV_DIM):
        cv_ref[:, c:c + CONV_DIM] = mm(C_CV + c, C_CV + c + CONV_DIM)
    for c in range(0, SB_AUG, 512):
        qa_ref[:, c:c + 512] = mm(C_QA + c, C_QA + c + 512).astype(BF16)
        ka_ref[:, c:c + 512] = mm(C_KA + c, C_KA + c + 512).astype(BF16)
    v = mm(C_SV, C_SV + SB_WIDTH)
    vb_ref[...] = v.astype(BF16)
    if kv_transposed:
        for r, ref in ((0, k_ref), (SB_WIDTH, v_ref)):
            ref[...] = lax.dot_general(wt_ref[r:r + SB_WIDTH, :], x, (((1,), (1,)), ((), ())),
                                       preferred_element_type=F32) + bt_ref[r:r + SB_WIDTH, :]
    else:
        k_ref[...] = mm(C_SK, C_SK + SB_WIDTH)
        v_ref[...] = v
    g_ref[...] = mm(C_G, C_G + LANES)


def _proj(x, w, b, wt, bt, layer, tm, nseq, kv_transposed):
    m = x.shape[0]
    t = m // nseq
    row = lambda width: pl.BlockSpec((tm, width), lambda i: (i, 0))
    outs = [(ML_COLS, BF16), (ML_WIDTH, F32), (3 * CONV_DIM, F32), (SB_AUG, BF16), (SB_AUG, BF16),
            (SB_WIDTH, BF16), (SB_WIDTH, F32), (SB_WIDTH, F32), (LANES, F32)]
    out_specs = [row(wd) for wd, _ in outs]
    out_shape = [jax.ShapeDtypeStruct((m, wd), dt) for wd, dt in outs]
    if kv_transposed:
        tiles = t // tm
        for idx in (6, 7):
            out_specs[idx] = pl.BlockSpec((None, SB_WIDTH, tm), lambda i: (i // tiles, 0, i % tiles))
            out_shape[idx] = jax.ShapeDtypeStruct((nseq, SB_WIDTH, t), F32)
    return pl.pallas_call(
        functools.partial(_proj_kernel, kv_transposed=kv_transposed),
        grid=(m // tm,),
        in_specs=[row(D_MODEL),
                  _layer_spec((D_MODEL, PROJ_COLS), layer, pipeline_mode=pl.Buffered(1)),
                  _layer_spec((1, PROJ_COLS), layer),
                  _layer_spec((2 * SB_WIDTH, D_MODEL), layer), _layer_spec((2 * SB_WIDTH, 1), layer)],
        out_specs=out_specs,
        out_shape=out_shape,
        compiler_params=_cparams(("arbitrary",), 56),
        name="proj",
    )(x, w, b, wt, bt)


def _mlstm_kernel(ml_ref, mo_ref, g_ref, c0_ref, m0_ref, nw_ref, up_ref, h_ref, c_ref, m_ref, cst, mst,
                  *, chunk, valid_len, nb):
    L = chunk

    @pl.when(pl.program_id(1) == 0)
    def _():
        cst[...] = c0_ref[...]
        mst[...] = m0_ref[...]

    up = up_ref[...]
    row_i = lax.broadcasted_iota(jnp.int32, (L, L), 0)
    col_i = lax.broadcasted_iota(jnp.int32, (L, L), 1)
    causal = row_i >= col_i
    lane_m = lax.broadcasted_iota(jnp.int32, (1, LANES), 1)

    a_rows, cols_all, m_prev_all = [], [], []
    for s in range(nb):
        gt = g_ref[s].T
        t_ig = gt[0:SUBLANES, :]
        t_f = gt[SUBLANES:2 * SUBLANES, :]
        lf = -_softplus(-t_f)
        if valid_len < L:
            pos = lax.broadcasted_iota(jnp.int32, (SUBLANES, L), 1)
            t_ig = jnp.where(pos < valid_len, t_ig, -1e30)
            lf = jnp.where(pos < valid_len, lf, 0.0)
        hi = lf.astype(BF16)
        r1 = lf - hi.astype(F32)
        mid = r1.astype(BF16)
        lo = (r1 - mid.astype(F32)).astype(BF16)
        b_row = (jnp.dot(hi, up, preferred_element_type=F32) + jnp.dot(mid, up, preferred_element_type=F32)
                 + jnp.dot(lo, up, preferred_element_type=F32))
        a_row = t_ig - b_row
        a_rows.append(a_row)
        cols_all.append(jnp.concatenate([a_row, b_row, jnp.zeros((LANES - 2 * SUBLANES, L), F32)], axis=0).T)
        m_prev_all.append(mst[s])

    chains = [(s, h) for s in range(nb) for h in range(ML_HEADS)]
    grp = lambda h, part: slice(3 * LANES * h + part * LANES, 3 * LANES * h + (part + 1) * LANES)
    qs = [ml_ref[s, :, grp(h, 0)] for s, h in chains]
    ks = [ml_ref[s, :, grp(h, 1)] for s, h in chains]
    vs = [ml_ref[s, :, grp(h, 2)] for s, h in chains]
    c_old = [cst[s, h] for s, h in chains]
    s_all = [lax.dot_general(q, k, (((1,), (1,)), ((), ())), preferred_element_type=F32) for q, k in zip(qs, ks)]
    qc_all = [jnp.dot(q, c.astype(BF16), preferred_element_type=F32) for q, c in zip(qs, c_old)]

    w_all, kw_all, inter_all, mt_all, decay_all = [], [], [], [], []
    m_next = list(m_prev_all)
    for c, (s, h) in enumerate(chains):
        a_c = cols_all[s][:, h:h + 1]
        b_c = cols_all[s][:, SUBLANES + h:SUBLANES + h + 1]
        a_r = a_rows[s][h:h + 1, :]
        m_prev = m_prev_all[s][:, h:h + 1]
        log_d = jnp.where(causal, b_c + a_r, -jnp.inf)
        m_loc = jnp.max(log_d, axis=1, keepdims=True)
        m_inter = b_c + m_prev
        m_t = jnp.maximum(m_inter, m_loc)
        d = jnp.exp(log_d - m_t)
        inter_all.append(jnp.exp(m_inter - m_t))
        mt_all.append(m_t)
        w_all.append((s_all[c] * d).astype(BF16))
        m_new = m_t[L - 1:L, :]
        b_last = b_c[L - 1:L, :]
        w_s = jnp.exp(b_last + a_c - m_new)
        decay_all.append(jnp.exp(b_last + m_prev - m_new))
        kw_all.append((ks[c].astype(F32) * w_s).astype(BF16))
        m_next[s] = jnp.where(lane_m == h, m_new, m_next[s])

    wv_all = [jnp.dot(w, v, preferred_element_type=F32) for w, v in zip(w_all, vs)]
    upd_all = [lax.dot_general(kw, v, (((0,), (0,)), ((), ())), preferred_element_type=F32)
               for kw, v in zip(kw_all, vs)]

    for c, (s, h) in enumerate(chains):
        nd = wv_all[c] + inter_all[c] * qc_all[c]
        num = nd[:, 0:HEAD_DIM]
        den = nd[:, HEAD_DIM:HEAD_DIM + 1]
        hh = num / jnp.maximum(jnp.abs(den), jnp.exp(-mt_all[c]))
        mu = jnp.mean(hh, axis=-1, keepdims=True)
        dh = hh - mu
        var = jnp.mean(dh * dh, axis=-1, keepdims=True)
        o = mo_ref[s, :, HEAD_DIM * h:HEAD_DIM * (h + 1)]
        gate = 1.0 / (1.0 + jnp.exp(-o))
        hn = dh * lax.rsqrt(var + LN_EPS) * nw_ref[:, HEAD_DIM * h:HEAD_DIM * (h + 1)] * gate
        h_ref[s, :, HEAD_DIM * h:HEAD_DIM * (h + 1)] = hn.astype(BF16)
        cst[s, h] = decay_all[c] * c_old[c] + upd_all[c]

    for s in range(nb):
        mst[s] = m_next[s]
        m_ref[s] = m_next[s]
    c_ref[...] = cst[...]


def _mlstm(ml, mo, g, c0, m0, nw, layer, nseq, chunk, valid_len, nb):
    m = ml.shape[0]
    t = m // nseq
    nc = t // chunk
    up = (lax.broadcasted_iota(jnp.int32, (chunk, chunk), 0)
          <= lax.broadcasted_iota(jnp.int32, (chunk, chunk), 1)).astype(BF16)
    row = lambda width: pl.BlockSpec((nb, chunk, width), lambda b, j: (b, j, 0))
    st_c = pl.BlockSpec((nb, ML_HEADS, LANES, LANES), lambda b, j: (b, 0, 0, 0))
    st_m = pl.BlockSpec((nb, 1, LANES), lambda b, j: (b, 0, 0))
    seq3 = lambda a: a.reshape(nseq, t, a.shape[-1])
    h, c_new, m_new = pl.pallas_call(
        functools.partial(_mlstm_kernel, chunk=chunk, valid_len=valid_len, nb=nb),
        grid=(nseq // nb, nc),
        in_specs=[row(ML_COLS), row(ML_WIDTH), row(LANES), st_c, st_m, _layer_spec((1, ML_WIDTH), layer),
                  _const_spec((chunk, chunk))],
        out_specs=[row(ML_WIDTH), st_c, st_m],
        out_shape=[jax.ShapeDtypeStruct((nseq, t, ML_WIDTH), BF16),
                   jax.ShapeDtypeStruct((nseq, ML_HEADS, LANES, LANES), F32),
                   jax.ShapeDtypeStruct((nseq, 1, LANES), F32)],
        scratch_shapes=[pltpu.VMEM((nb, ML_HEADS, LANES, LANES), F32), pltpu.VMEM((nb, 1, LANES), F32)],
        compiler_params=_cparams(("arbitrary", "arbitrary"), 40),
        name="mlstm",
    )(seq3(ml), seq3(mo), seq3(g), c0, m0, nw, up)
    return h.reshape(m, ML_WIDTH), c_new, m_new


def _sb_prompt_kernel(q_ref, k_ref, v_ref, tri_ref, o_ref, acc_ref, carry_ref, *, tq, tk):
    i = pl.program_id(2)
    nkb = tq // tk
    tri = tri_ref[...]
    acc_ref[...] = jnp.zeros_like(acc_ref)
    carry_ref[...] = jnp.zeros_like(carry_ref)

    def steps(items):
        chains = [(n, hh) for n in range(len(items)) for hh in range(2)]
        starts = [pl.multiple_of(j * tk, tk) for j, _, _, _ in items]
        keep = []
        for j, r0, nr, masked in items:
            if masked:
                qpos = i * tq + r0 + lax.broadcasted_iota(jnp.int32, (nr, tk), 0)
                kpos = j * tk + lax.broadcasted_iota(jnp.int32, (nr, tk), 1)
                keep.append(kpos < qpos)
            else:
                keep.append(None)
        z = [lax.dot_general(q_ref[items[n][1]:items[n][1] + items[n][2], hh * LANES:(hh + 1) * LANES],
                             k_ref[pl.ds(starts[n], tk), hh * LANES:(hh + 1) * LANES],
                             (((1,), (1,)), ((), ())), preferred_element_type=F32) for n, hh in chains]
        cum = []
        for c, (n, hh) in enumerate(chains):
            sp = _softplus2(z[c])
            if keep[n] is not None:
                sp = jnp.where(keep[n], sp, 0.0)
            cum.append(jnp.dot(sp.astype(BF16), tri, preferred_element_type=F32))
        pv = []
        for c, (n, hh) in enumerate(chains):
            a = jnp.exp2(z[c] + cum[c])
            if keep[n] is not None:
                a = jnp.where(keep[n], a, 0.0)
            pv.append(jnp.dot(a.astype(BF16), v_ref[pl.ds(starts[n], tk), :], preferred_element_type=F32))
        for c, (n, hh) in enumerate(chains):
            rows = slice(items[n][1], items[n][1] + items[n][2])
            carry = carry_ref[hh, rows]
            acc_ref[hh, rows] += jnp.exp2(carry) * pv[c]
            carry_ref[hh, rows] = carry + cum[c][:, 0:1]

    assert nkb == 2
    steps([(i * nkb + 1, tk, tk, True), (i * nkb, 0, tk, True), (i * nkb, tk, tk, False)])
    full = lambda j: (j, 0, tq, False)

    top = i * nkb - 1
    one = i % 2

    @pl.when(one == 1)
    def _():
        steps([full(top - n) for n in range(2)])

    def body(t, _):
        j0 = top - 2 * one - 4 * t
        steps([full(j0 - n) for n in range(4)])
        return 0

    lax.fori_loop(0, i // 2, body, 0)

    lane = lax.broadcasted_iota(jnp.int32, (tq, LANES), 1)
    o_ref[...] = jnp.where(lane >= HEAD_DIM, acc_ref[1], acc_ref[0]).astype(BF16)


def _sb_prompt(qa, ka, vb, nseq, tq, tk):
    m = qa.shape[0]
    t = m // nseq
    nq = t // tq
    tri = -(lax.broadcasted_iota(jnp.int32, (tk, tk), 0)
            >= lax.broadcasted_iota(jnp.int32, (tk, tk), 1)).astype(BF16)
    return pl.pallas_call(
        functools.partial(_sb_prompt_kernel, tq=tq, tk=tk),
        grid=(nseq, SB_HEADS // 2, nq),
        in_specs=[pl.BlockSpec((tq, 2 * LANES), lambda b, hp, i: (b * nq + i, hp)),
                  pl.BlockSpec((t, 2 * LANES), lambda b, hp, i: (b, hp)),
                  pl.BlockSpec((t, LANES), lambda b, hp, i: (b, hp)),
                  pl.BlockSpec((tk, tk), lambda b, hp, i: (0, 0))],
        out_specs=pl.BlockSpec((tq, LANES), lambda b, hp, i: (b * nq + i, hp)),
        out_shape=jax.ShapeDtypeStruct((m, SB_WIDTH), BF16),
        scratch_shapes=[pltpu.VMEM((2, tq, LANES), F32), pltpu.VMEM((2, tq, 1), F32)],
        compiler_params=_cparams(("arbitrary", "arbitrary", "arbitrary"), 40),
        name="sb_prompt",
    )(qa, ka, vb, tri)


def _sb_sample_kernel(pt_ref, qbd_ref, bias_ref, kn_ref, vn_ref, tri_ref, *refs, pps, nq):
    k_refs = refs[0:pps]
    v_refs = refs[pps:2 * pps]
    o_ref = refs[2 * pps]
    acc_ref, carry_ref = refs[2 * pps + 1:]
    j = pl.program_id(1)
    qbd = qbd_ref[0]
    bias = bias_ref[...]
    tri = tri_ref[...]

    def blocks(kts, vts, keep):
        z = [jnp.dot(qbd, kt[...].astype(BF16), preferred_element_type=F32) + bias for kt in kts]
        cum = []
        for zc in z:
            sp = _softplus2(zc)
            if keep is not None:
                sp = jnp.where(keep, sp, 0.0)
            cum.append(jnp.dot(sp.astype(BF16), tri, preferred_element_type=F32))
        pv = []
        for zc, cc, vt in zip(z, cum, vts):
            a = jnp.exp2(zc + cc)
            if keep is not None:
                a = jnp.where(keep, a, 0.0)
            pv.append(lax.dot_general(a.astype(BF16), vt[...].astype(BF16), (((1,), (1,)), ((), ())),
                                      preferred_element_type=F32))
        acc = acc_ref[...]
        carry = carry_ref[...]
        for pc, cc in zip(pv, cum):
            acc = acc + jnp.exp2(carry) * pc
            carry = carry + cc[:, 0:1]
        acc_ref[...] = acc
        carry_ref[...] = carry

    @pl.when(j == 0)
    def _():
        acc_ref[...] = jnp.zeros_like(acc_ref)
        carry_ref[...] = jnp.zeros_like(carry_ref)
        t_i = lax.broadcasted_iota(jnp.int32, (SB_HEADS * nq, PAGE_SIZE), 0) % nq
        s_i = lax.broadcasted_iota(jnp.int32, (SB_HEADS * nq, PAGE_SIZE), 1)
        blocks([kn_ref.at[0]], [vn_ref.at[0]], s_i < t_i)

    blocks(k_refs, v_refs, None)

    @pl.when(j == pl.num_programs(1) - 1)
    def _():
        acc = acc_ref[...]
        lane_h = lax.broadcasted_iota(jnp.int32, (nq, SB_WIDTH), 1) // HEAD_DIM
        res = jnp.zeros((nq, SB_WIDTH), F32)
        for h in range(SB_HEADS):
            res = jnp.where(lane_h == h, acc[nq * h:nq * (h + 1), :], res)
        o_ref[0] = res


def _sb_sample(qbd, bias, kt_new, vt_new, cache_kt, cache_vt, page_table, layer, nq):
    bsz, n_pages = page_table.shape
    pps = PAGES_PER_STEP
    steps = n_pages // pps
    rows = SB_HEADS * nq
    tri = -(lax.broadcasted_iota(jnp.int32, (PAGE_SIZE, PAGE_SIZE), 0)
            >= lax.broadcasted_iota(jnp.int32, (PAGE_SIZE, PAGE_SIZE), 1)).astype(BF16)

    def page_spec(p):
        def imap(b, j, pt):
            return (layer, pt[b * n_pages + (n_pages - 1 - (j * pps + p))], 0, 0)
        return pl.BlockSpec((None, None, SB_WIDTH, PAGE_SIZE), imap)

    per_seq = lambda shape: pl.BlockSpec((1,) + shape, lambda b, j, pt: (b, 0, 0))
    grid_spec = pltpu.PrefetchScalarGridSpec(
        num_scalar_prefetch=1,
        grid=(bsz, steps),
        in_specs=[per_seq((rows, SB_WIDTH)), pl.BlockSpec((rows, PAGE_SIZE), lambda b, j, pt: (0, 0)),
                  per_seq((SB_WIDTH, PAGE_SIZE)), per_seq((SB_WIDTH, PAGE_SIZE)),
                  pl.BlockSpec((PAGE_SIZE, PAGE_SIZE), lambda b, j, pt: (0, 0))]
                 + [page_spec(p) for p in range(pps)] + [page_spec(p) for p in range(pps)],
        out_specs=per_seq((nq, SB_WIDTH)),
        scratch_shapes=[pltpu.VMEM((rows, SB_WIDTH), F32), pltpu.VMEM((rows, 1), F32)],
    )
    return pl.pallas_call(
        functools.partial(_sb_sample_kernel, pps=pps, nq=nq),
        grid_spec=grid_spec,
        out_shape=jax.ShapeDtypeStruct((bsz, nq, SB_WIDTH), F32),
        compiler_params=_cparams(("arbitrary", "arbitrary"), 40),
        name="sb_sample",
    )(page_table.reshape(-1), qbd, bias, kt_new, vt_new, tri, *([cache_kt] * pps), *([cache_vt] * pps))


def _conv3(p, s_ref, cs, w_ref, tm, halo):
    s_ref[SUBLANES:SUBLANES + tm, cs] = p
    p1 = s_ref[SUBLANES - 1:SUBLANES - 1 + tm, cs]
    p2 = s_ref[SUBLANES - 2:SUBLANES - 2 + tm, cs]
    if halo is not None:
        e1, e2, seq_len = halo
        tpos = lax.broadcasted_iota(jnp.int32, p.shape, 0) % seq_len
        p1 = jnp.where(tpos == 0, e1, p1)
        p2 = jnp.where(tpos < 2, e2, p2)
    return w_ref[0:1, cs] * p2 + w_ref[1:2, cs] * p1 + w_ref[2:3, cs] * p


def _start_tile(s_ref, tiles_per_seq):
    @pl.when(pl.program_id(0) % tiles_per_seq == 0)
    def _():
        s_ref[0:SUBLANES, :] = jnp.zeros((SUBLANES, s_ref.shape[1]), F32)


def _carry_rows(s_ref, tm):
    s_ref[0:SUBLANES, :] = s_ref[tm:tm + SUBLANES, :]


def _outproj_kernel(*refs, tm, tiles_per_seq, seq_len):
    sample = seq_len < tm
    if sample:
        hml_ref, cv_ref, hsb_ref, x_ref, w_ref, cw_ref, g_ref, b_ref, e1_ref, e2_ref, y_ref, tail_ref, s_ref = refs
        halo = (e1_ref[...], e2_ref[...], seq_len)
    else:
        hml_ref, cv_ref, hsb_ref, x_ref, w_ref, cw_ref, g_ref, b_ref, y_ref, tail_ref, s_ref = refs
        halo = None
    _start_tile(s_ref, tiles_per_seq)
    cb = cv_ref[:, 0:CONV_DIM]
    p = cv_ref[:, CONV_DIM:2 * CONV_DIM] * cv_ref[:, 2 * CONV_DIM:3 * CONV_DIM]
    u = _conv3(p, s_ref, slice(None), cw_ref, tm, halo)
    if sample:
        tail_ref[...] = p
    else:
        tail_ref[...] = p[tm - SUBLANES:tm, :]
        _carry_rows(s_ref, tm)
    h_conv = (cb * u).astype(BF16)
    mix = (jnp.dot(hml_ref[...], w_ref[0:ML_WIDTH, :], preferred_element_type=F32)
           + jnp.dot(h_conv, w_ref[ML_WIDTH:ML_WIDTH + CONV_DIM, :], preferred_element_type=F32)
           + jnp.dot(hsb_ref[...].astype(BF16), w_ref[ML_WIDTH + CONV_DIM:, :], preferred_element_type=F32))
    y_ref[...] = _layer_norm(DN_ALPHA * x_ref[...] + mix, g_ref[...], b_ref[...])


def _outproj(hml, cv, hsb, x, w, cw, g, b, layer, tm, seq_len, halo=None):
    m = x.shape[0]
    sample = seq_len < tm
    tiles_per_seq = max(seq_len // tm, 1)
    row = lambda width: pl.BlockSpec((tm, width), lambda i: (i, 0))
    tail_rows = tm if sample else SUBLANES
    in_specs = [row(ML_WIDTH), row(3 * CONV_DIM), row(SB_WIDTH), row(D_MODEL),
                _layer_spec((D_MODEL, D_MODEL), layer), _layer_spec((SUBLANES, CONV_DIM), layer),
                _layer_spec((1, D_MODEL), layer), _layer_spec((1, D_MODEL), layer)]
    args = [hml, cv, hsb, x, w, cw, g, b]
    if sample:
        in_specs += [row(CONV_DIM), row(CONV_DIM)]
        args += list(halo)
    return pl.pallas_call(
        functools.partial(_outproj_kernel, tm=tm, tiles_per_seq=tiles_per_seq, seq_len=seq_len),
        grid=(m // tm,),
        in_specs=in_specs,
        out_specs=[row(D_MODEL), pl.BlockSpec((tail_rows, CONV_DIM), lambda i: (i, 0))],
        out_shape=[jax.ShapeDtypeStruct((m, D_MODEL), F32),
                   jax.ShapeDtypeStruct((m // tm * tail_rows, CONV_DIM), F32)],
        scratch_shapes=[pltpu.VMEM((tm + SUBLANES, CONV_DIM), F32)],
        compiler_params=_cparams(("arbitrary",), 40),
        name="outproj",
    )(*args)


def _ffn_kernel(*refs, tm, tiles_per_seq, seq_len):
    sample = seq_len < tm
    if sample:
        x_ref, wu_ref, cw_ref, wd_ref, g_ref, b_ref, e1_ref, e2_ref, y_ref, tail_ref, s_ref, acc_ref = refs
    else:
        x_ref, wu_ref, cw_ref, wd_ref, g_ref, b_ref, y_ref, tail_ref, s_ref, acc_ref = refs
    _start_tile(s_ref, tiles_per_seq)
    x = x_ref[...]
    xb = x.astype(BF16)
    n_chunks = D_FF // FF_CHUNK

    def up(c):
        return (jnp.dot(xb, wu_ref[:, c * FF_CHUNK:(c + 1) * FF_CHUNK], preferred_element_type=F32),
                jnp.dot(xb, wu_ref[:, D_FF + c * FF_CHUNK:D_FF + (c + 1) * FF_CHUNK], preferred_element_type=F32))

    nxt = up(0)
    for c in range(n_chunks):
        cs = slice(c * FF_CHUNK, (c + 1) * FF_CHUNK)
        g_pre, val = nxt
        if c + 1 < n_chunks:
            nxt = up(c + 1)
        halo = (e1_ref[:, cs], e2_ref[:, cs], seq_len) if sample else None
        g_conv = _conv3(g_pre, s_ref, cs, cw_ref, tm, halo)
        hid = (g_conv / (1.0 + jnp.exp(-g_conv)) * val).astype(BF16)
        part = jnp.dot(hid, wd_ref[cs, :], preferred_element_type=F32)
        if c == 0:
            acc_ref[...] = part
        else:
            acc_ref[...] += part
    if sample:
        tail_ref[...] = s_ref[SUBLANES:SUBLANES + tm, :]
    else:
        tail_ref[...] = s_ref[tm:tm + SUBLANES, :]
        _carry_rows(s_ref, tm)
    y_ref[...] = _layer_norm(DN_ALPHA * x + acc_ref[...], g_ref[...], b_ref[...])


def _ffn(x, wu, cw, wd, g, b, layer, tm, seq_len, halo=None):
    m = x.shape[0]
    sample = seq_len < tm
    tiles_per_seq = max(seq_len // tm, 1)
    row = lambda width: pl.BlockSpec((tm, width), lambda i: (i, 0))
    tail_rows = tm if sample else SUBLANES
    in_specs = [row(D_MODEL),
                _layer_spec((D_MODEL, 2 * D_FF), layer, pipeline_mode=pl.Buffered(1)),
                _layer_spec((SUBLANES, D_FF), layer),
                _layer_spec((D_FF, D_MODEL), layer, pipeline_mode=pl.Buffered(1)),
                _layer_spec((1, D_MODEL), layer), _layer_spec((1, D_MODEL), layer)]
    args = [x, wu, cw, wd, g, b]
    if sample:
        in_specs += [row(D_FF), row(D_FF)]
        args += list(halo)
    return pl.pallas_call(
        functools.partial(_ffn_kernel, tm=tm, tiles_per_seq=tiles_per_seq, seq_len=seq_len),
        grid=(m // tm,),
        in_specs=in_specs,
        out_specs=[row(D_MODEL), pl.BlockSpec((tail_rows, D_FF), lambda i: (i, 0))],
        out_shape=[jax.ShapeDtypeStruct((m, D_MODEL), F32),
                   jax.ShapeDtypeStruct((m // tm * tail_rows, D_FF), F32)],
        scratch_shapes=[pltpu.VMEM((tm + SUBLANES, D_FF), F32), pltpu.VMEM((tm, D_MODEL), F32)],
        compiler_params=_cparams(("arbitrary",), 56),
        name="ffn",
    )(*args)


def _split3(x):
    hi = x.astype(BF16).astype(F32)
    mid = (x - hi).astype(BF16).astype(F32)
    lo = (x - hi - mid).astype(BF16).astype(F32)
    return hi, mid, lo


def _pack_w_in(w_in, b_in, sb_bias):
    depth = w_in.shape[0]
    scale = HEAD_DIM ** -0.5
    b2 = _split3(sb_bias * LOG2E)
    zb = lambda n: jnp.zeros((depth, n), F32)
    gi = 4 * ML_WIDTH
    cv0 = gi + 2 * ML_HEADS
    sq0 = cv0 + 3 * CONV_DIM
    sk0 = sq0 + SB_WIDTH

    def head_groups(lo, nheads, factor=None):
        seg = w_in[:, :, lo:lo + HEAD_DIM * nheads].reshape(depth, D_MODEL, nheads, HEAD_DIM)
        if factor is not None:
            seg = seg * factor
        return jnp.pad(seg, ((0, 0), (0, 0), (0, 0), (0, LANES - HEAD_DIM)))

    ml = jnp.stack([head_groups(0, ML_HEADS), head_groups(ML_WIDTH, ML_HEADS, scale),
                    head_groups(2 * ML_WIDTH, ML_HEADS)], axis=3).reshape(depth, D_MODEL, ML_COLS)
    qa = head_groups(sq0, SB_HEADS, scale * LOG2E).reshape(depth, D_MODEL, SB_AUG)
    ka = head_groups(sk0, SB_HEADS).reshape(depth, D_MODEL, SB_AUG)
    gates = jnp.pad(w_in[:, :, gi:gi + 2 * ML_HEADS].reshape(depth, D_MODEL, 2, ML_HEADS),
                    ((0, 0), (0, 0), (0, 0), (0, SUBLANES - ML_HEADS))).reshape(depth, D_MODEL, 2 * SUBLANES)
    gates = jnp.pad(gates, ((0, 0), (0, 0), (0, LANES - 2 * SUBLANES)))
    w = jnp.concatenate([ml, w_in[:, :, 3 * ML_WIDTH:4 * ML_WIDTH], w_in[:, :, cv0:sq0], qa, ka, w_in[:, :, sk0:],
                         gates], axis=-1).astype(BF16)

    bp = []
    for h in range(ML_HEADS):
        lo, hi = HEAD_DIM * h, HEAD_DIM * (h + 1)
        bp += [b_in[:, lo:hi], zb(HEAD_DIM)]
        bp += [b_in[:, ML_WIDTH + lo:ML_WIDTH + hi] * scale, zb(HEAD_DIM)]
        bp += [b_in[:, 2 * ML_WIDTH + lo:2 * ML_WIDTH + hi], jnp.ones((depth, 1), F32), zb(HEAD_DIM - 1)]
    bp += [b_in[:, 3 * ML_WIDTH:4 * ML_WIDTH], b_in[:, cv0:sq0]]
    for h in range(SB_HEADS):
        lo, hi = sq0 + HEAD_DIM * h, sq0 + HEAD_DIM * (h + 1)
        bp += [b_in[:, lo:hi] * (scale * LOG2E)] + [part[:, h:h + 1] for part in b2] + [zb(HEAD_DIM - 3)]
    for h in range(SB_HEADS):
        lo, hi = sk0 + HEAD_DIM * h, sk0 + HEAD_DIM * (h + 1)
        bp += [b_in[:, lo:hi], jnp.ones((depth, 3), F32), zb(HEAD_DIM - 3)]
    bp += [b_in[:, sk0:],
           b_in[:, gi:gi + ML_HEADS], zb(SUBLANES - ML_HEADS),
           b_in[:, gi + ML_HEADS:gi + 2 * ML_HEADS], zb(LANES - SUBLANES - ML_HEADS)]
    b = jnp.concatenate(bp, axis=-1)[:, None, :]
    assert w.shape[-1] == PROJ_COLS and b.shape[-1] == PROJ_COLS
    wt = jnp.swapaxes(w_in[:, :, sk0:], 1, 2).astype(BF16)
    bt = b_in[:, sk0:, None]
    return w, b, wt, bt


def _pad_rows8(a):
    return jnp.pad(a, ((0, 0), (0, SUBLANES - a.shape[1]), (0, 0)))


def _ext_state(c, n):
    ext = jnp.concatenate([c, n[..., None]], axis=-1)
    return jnp.pad(ext, ((0, 0), (0, 0), (0, LANES - HEAD_DIM), (0, LANES - HEAD_DIM - 1)))


def _halo(prev, seq_len):
    bsz, _, c = prev.shape
    z = jnp.zeros((bsz, seq_len, c), F32)
    e1 = z.at[:, 0].set(prev[:, 1])
    e2 = z.at[:, 0].set(prev[:, 0]).at[:, 1].set(prev[:, 1])
    return e1.reshape(bsz * seq_len, c), e2.reshape(bsz * seq_len, c)


def _layer_prompt(x, nseq, wts, layer):
    (w_in, b_in, wt, bt, nw, cw, sbias, w_out, g1, b1, w_up, fcw, w_down, g2, b2) = wts
    m = x.shape[0]
    t = m // nseq
    ml, mo, cv, qa, ka, vb, skt, svt, g = _proj(x, w_in, b_in, wt, bt, layer, ROW_TILE, nseq, True)
    c0 = jnp.zeros((nseq, ML_HEADS, LANES, LANES), F32)
    m0 = jnp.zeros((nseq, 1, LANES), F32)
    hml, c_ext, m_new = _mlstm(ml, mo, g, c0, m0, nw, layer, nseq, MLSTM_CHUNK, MLSTM_CHUNK, MLSTM_SEQS)
    hsb = _sb_prompt(qa, ka, vb, nseq, ATT_TQ, ATT_TK)
    x1, ptail = _outproj(hml, cv, hsb, x, w_out, cw, g1, b1, layer, ROW_TILE, t)
    x2, gtail = _ffn(x1, w_up, fcw, w_down, g2, b2, layer, ROW_TILE, t)
    last = lambda tail: tail.reshape(nseq, t // ROW_TILE, SUBLANES, -1)[:, -1, SUBLANES - 2:, :]
    states = (skt, svt,
              c_ext[:, :, :HEAD_DIM, :HEAD_DIM], c_ext[:, :, :HEAD_DIM, HEAD_DIM], m_new[:, 0, :ML_HEADS],
              last(ptail), last(gtail))
    return x2, states


def _layer_sample(x, nseq, wts, layer, cache_kt, cache_vt, page_table, c_prev, n_prev, m_prev, conv_prev,
                  ffn_prev):
    (w_in, b_in, wt, bt, nw, cw, sbias, w_out, g1, b1, w_up, fcw, w_down, g2, b2) = wts
    m = x.shape[0]
    t = m // nseq
    ml, mo, cv, qa, _, _, sk, sv, g = _proj(x, w_in, b_in, wt, bt, layer, m, nseq, False)

    pad = lambda a: jnp.pad(a.reshape(nseq, t, -1), ((0, 0), (0, SAMPLE_PAD - t), (0, 0))).reshape(
        nseq * SAMPLE_PAD, -1)
    m0 = jnp.pad(m_prev, ((0, 0), (0, LANES - ML_HEADS)))[:, None, :]
    hml, c_ext, m_new = _mlstm(pad(ml), pad(mo), pad(g), _ext_state(c_prev, n_prev), m0, nw, layer, nseq,
                               SAMPLE_PAD, t, MLSTM_SEQS)
    hml = hml.reshape(nseq, SAMPLE_PAD, ML_WIDTH)[:, :t].reshape(m, ML_WIDTH)

    q3 = qa.reshape(nseq, t, SB_HEADS, LANES)[..., :HEAD_DIM]
    eye = jnp.eye(SB_HEADS, dtype=BF16)
    qbd = jnp.einsum("bthd,hg->bhtgd", q3, eye).reshape(nseq, SB_HEADS * t, SB_WIDTH)
    bias = jnp.broadcast_to(jnp.repeat(sbias[layer] * LOG2E, t)[:, None], (SB_HEADS * t, PAGE_SIZE))
    new_t = lambda a: jnp.pad(jnp.swapaxes(a.reshape(nseq, t, SB_WIDTH), 1, 2),
                              ((0, 0), (0, 0), (0, PAGE_SIZE - t)))
    hsb = _sb_sample(qbd, bias, new_t(sk), new_t(sv), cache_kt, cache_vt, page_table, layer, t).reshape(
        m, SB_WIDTH)

    x1, p_all = _outproj(hml, cv, hsb, x, w_out, cw, g1, b1, layer, m, t, _halo(conv_prev, t))
    x2, g_all = _ffn(x1, w_up, fcw, w_down, g2, b2, layer, m, t, _halo(ffn_prev, t))
    last = lambda a: a.reshape(nseq, t, -1)[:, t - 2:, :]
    states = (sk.reshape(nseq, t, SB_HEADS, HEAD_DIM), sv.reshape(nseq, t, SB_HEADS, HEAD_DIM),
              c_ext[:, :, :HEAD_DIM, :HEAD_DIM], c_ext[:, :, :HEAD_DIM, HEAD_DIM], m_new[:, 0, :ML_HEADS],
              last(p_all), last(g_all))
    return x2, states


def _feature_major_pages(cache):
    d, p = cache.shape[:2]
    return jnp.transpose(cache, (0, 1, 3, 4, 2)).reshape(d, p, SB_WIDTH, PAGE_SIZE)


def _token_major_state(kt):
    d, b, _, t = kt.shape
    return jnp.transpose(kt.reshape(d, b, SB_HEADS, HEAD_DIM, t), (0, 1, 4, 2, 3))


def kernel(x_prompt, x_sample, cache_k, cache_v, state_mlstm_c, state_mlstm_n, state_mlstm_m, state_conv,
           state_ffn_conv, page_table, w_in, b_in, mlstm_norm_w, conv_w, sb_bias, w_out, ln1_g, ln1_b, ffn_w_up,
           ffn_conv_w, ffn_w_down, ln2_g, ln2_b):
    bp, tp, _ = x_prompt.shape
    bs, ts, _ = x_sample.shape
    depth = w_in.shape[0]
    w_in_p, b_in_p, wt_p, bt_p = _pack_w_in(w_in, b_in, sb_bias)
    w_out_b = w_out.astype(BF16)
    w_up_b = ffn_w_up.astype(BF16)
    w_down_b = ffn_w_down.astype(BF16)
    cw_p = _pad_rows8(conv_w)
    fcw_p = _pad_rows8(ffn_conv_w)
    ckt = _feature_major_pages(cache_k)
    cvt = _feature_major_pages(cache_v)

    yp = x_prompt.reshape(bp * tp, D_MODEL)
    ys = x_sample.reshape(bs * ts, D_MODEL)
    st_p = [[] for _ in range(7)]
    st_s = [[] for _ in range(7)]
    row3 = lambda a: a[:, None, :]
    wts = (w_in_p, b_in_p, wt_p, bt_p, row3(mlstm_norm_w), cw_p, sb_bias, w_out_b, row3(ln1_g), row3(ln1_b),
           w_up_b, fcw_p, w_down_b, row3(ln2_g), row3(ln2_b))
    for l in range(depth):
        yp, new_p = _layer_prompt(yp, bp, wts, l)
        ys, new_s = _layer_sample(ys, bs, wts, l, ckt, cvt, page_table, state_mlstm_c[l], state_mlstm_n[l],
                                  state_mlstm_m[l], state_conv[l], state_ffn_conv[l])
        for lst, a in zip(st_p, new_p):
            lst.append(a)
        for lst, a in zip(st_s, new_s):
            lst.append(a)
    outs_p = [jnp.stack(s) for s in st_p]
    outs_p[0] = _token_major_state(outs_p[0])
    outs_p[1] = _token_major_state(outs_p[1])
    outs_s = [jnp.stack(s) for s in st_s]
    return (yp.reshape(bp, tp, D_MODEL), ys.reshape(bs, ts, D_MODEL), *outs_p, *outs_s)
```

```python
import functools

import jax
import jax.numpy as jnp
from jax import lax
from jax.experimental import pallas as pl
from jax.experimental.pallas import tpu as pltpu

F32 = jnp.float32
BF16 = jnp.bfloat16

D_MODEL = 1024
DEPTH = 4
ML_HEADS = 4
HEAD_DIM = 64
ML_WIDTH = ML_HEADS * HEAD_DIM
CONV_DIM = 256
CONV_WIDTH = 3
SB_HEADS = 8
SB_WIDTH = SB_HEADS * HEAD_DIM
D_FF = 2816
PAGE_SIZE = 128
LN_EPS = 1e-5
DN_ALPHA = (2 * DEPTH) ** 0.25

LANES = 128
SUBLANES = 8
MIB = 1024 * 1024

C_ML = 0
ML_COLS = ML_HEADS * 3 * LANES
C_MO = C_ML + ML_COLS
C_CV = C_MO + ML_WIDTH
SB_AUG = SB_HEADS * LANES
C_QA = C_CV + 3 * CONV_DIM
C_KA = C_QA + SB_AUG
C_SK = C_KA + SB_AUG
C_SV = C_SK + SB_WIDTH
C_G = C_SV + SB_WIDTH
PROJ_COLS = C_G + LANES
LOG2E = 1.4426950408889634

MLSTM_CHUNK = 512
MLSTM_SEQS = 1
SAMPLE_PAD = 128
ROW_TILE = 512
ATT_TQ = 512
ATT_TK = 256
PAGES_PER_STEP = 16
FF_CHUNK = 256


def _cparams(sem, vmem_mib):
    return pltpu.CompilerParams(dimension_semantics=sem, vmem_limit_bytes=vmem_mib * MIB)


def _const_spec(shape):
    zeros = (0,) * len(shape)
    return pl.BlockSpec(shape, lambda *_: zeros)


def _layer_spec(shape, layer, **kwargs):
    index = (layer,) + (0,) * len(shape)
    return pl.BlockSpec((None,) + tuple(shape), lambda *_: index, **kwargs)


def _softplus(z):
    return jnp.maximum(z, 0.0) + jnp.log(1.0 + jnp.exp(-jnp.abs(z)))


def _softplus2(z):
    sign_bit = jnp.uint32(0x80000000)
    neg_abs = lax.bitcast_convert_type(lax.bitcast_convert_type(z, jnp.uint32) | sign_bit, F32)
    return jnp.maximum(z, 0.0) + jnp.log2(1.0 + jnp.exp2(neg_abs))


def _layer_norm(y, g, b):
    mu = jnp.mean(y, axis=-1, keepdims=True)
    d = y - mu
    var = jnp.mean(d * d, axis=-1, keepdims=True)
    return d * lax.rsqrt(var + LN_EPS) * g + b


def _proj_kernel(x_ref, w_ref, b_ref, wt_ref, bt_ref, ml_ref, mo_ref, cv_ref, qa_ref, ka_ref, vb_ref, k_ref, v_ref,
                 g_ref, *, kv_transposed):
    x = x_ref[...].astype(BF16)

    def mm(c0, c1):
        return jnp.dot(x, w_ref[:, c0:c1], preferred_element_type=F32) + b_ref[:, c0:c1]

    for c in range(0, ML_COLS, 512):
        ml_ref[:, c:c + 512] = mm(C_ML + c, C_ML + c + 512).astype(BF16)
    mo_ref[...] = mm(C_MO, C_MO + ML_WIDTH)
    for c in range(0, 3 * CONV_DIM, CONV_DIM):
        cv_ref[:, c:c + CONV_DIM] = mm(C_CV + c, C_CV + c + CONV_DIM)
    for c in range(0, SB_AUG, 512):
        qa_ref[:, c:c + 512] = mm(C_QA + c, C_QA + c + 512).astype(BF16)
        ka_ref[:, c:c + 512] = mm(C_KA + c, C_KA + c + 512).astype(BF16)
    v = mm(C_SV, C_SV + SB_WIDTH)
    vb_ref[...] = v.astype(BF16)
    if kv_transposed:
        for r, ref in ((0, k_ref), (SB_WIDTH, v_ref)):
            ref[...] = lax.dot_general(wt_ref[r:r + SB_WIDTH, :], x, (((1,), (1,)), ((), ())),
                                       preferred_element_type=F32) + bt_ref[r:r + SB_WIDTH, :]
    else:
        k_ref[...] = mm(C_SK, C_SK + SB_WIDTH)
        v_ref[...] = v
    g_ref[...] = mm(C_G, C_G + LANES)


def _proj(x, w, b, wt, bt, layer, tm, nseq, kv_transposed):
    m = x.shape[0]
    t = m // nseq
    row = lambda width: pl.BlockSpec((tm, width), lambda i: (i, 0))
    outs = [(ML_COLS, BF16), (ML_WIDTH, F32), (3 * CONV_DIM, F32), (SB_AUG, BF16), (SB_AUG, BF16),
            (SB_WIDTH, BF16), (SB_WIDTH, F32), (SB_WIDTH, F32), (LANES, F32)]
    out_specs = [row(wd) for wd, _ in outs]
    out_shape = [jax.ShapeDtypeStruct((m, wd), dt) for wd, dt in outs]
    if kv_transposed:
        tiles = t // tm
        for idx in (6, 7):
            out_specs[idx] = pl.BlockSpec((None, SB_WIDTH, tm), lambda i: (i // tiles, 0, i % tiles))
            out_shape[idx] = jax.ShapeDtypeStruct((nseq, SB_WIDTH, t), F32)
    return pl.pallas_call(
        functools.partial(_proj_kernel, kv_transposed=kv_transposed),
        grid=(m // tm,),
        in_specs=[row(D_MODEL),
                  _layer_spec((D_MODEL, PROJ_COLS), layer, pipeline_mode=pl.Buffered(1)),
                  _layer_spec((1, PROJ_COLS), layer),
                  _layer_spec((2 * SB_WIDTH, D_MODEL), layer), _layer_spec((2 * SB_WIDTH, 1), layer)],
        out_specs=out_specs,
        out_shape=out_shape,
        compiler_params=_cparams(("arbitrary",), 56),
        name="proj",
    )(x, w, b, wt, bt)


def _mlstm_kernel(ml_ref, mo_ref, g_ref, c0_ref, m0_ref, nw_ref, up_ref, h_ref, c_ref, m_ref, cst, mst,
                  *, chunk, valid_len, nb):
    L = chunk

    @pl.when(pl.program_id(1) == 0)
    def _():
        cst[...] = c0_ref[...]
        mst[...] = m0_ref[...]

    up = up_ref[...]
    row_i = lax.broadcasted_iota(jnp.int32, (L, L), 0)
    col_i = lax.broadcasted_iota(jnp.int32, (L, L), 1)
    causal = row_i >= col_i
    lane_m = lax.broadcasted_iota(jnp.int32, (1, LANES), 1)

    a_rows, cols_all, m_prev_all = [], [], []
    for s in range(nb):
        gt = g_ref[s].T
        t_ig = gt[0:SUBLANES, :]
        t_f = gt[SUBLANES:2 * SUBLANES, :]
        lf = -_softplus(-t_f)
        if valid_len < L:
            pos = lax.broadcasted_iota(jnp.int32, (SUBLANES, L), 1)
            t_ig = jnp.where(pos < valid_len, t_ig, -1e30)
            lf = jnp.where(pos < valid_len, lf, 0.0)
        hi = lf.astype(BF16)
        r1 = lf - hi.astype(F32)
        mid = r1.astype(BF16)
        lo = (r1 - mid.astype(F32)).astype(BF16)
        b_row = (jnp.dot(hi, up, preferred_element_type=F32) + jnp.dot(mid, up, preferred_element_type=F32)
                 + jnp.dot(lo, up, preferred_element_type=F32))
        a_row = t_ig - b_row
        a_rows.append(a_row)
        pos_l = lax.broadcasted_iota(jnp.int32, (SUBLANES, L), 1)
        cmax = a_row
        shift = 1
        while shift < L:
            cmax = jnp.maximum(cmax, jnp.where(pos_l >= shift, pltpu.roll(cmax, shift, axis=1), -jnp.inf))
            shift *= 2
        cols_all.append(jnp.concatenate([a_row, b_row, cmax, jnp.zeros((LANES - 3 * SUBLANES, L), F32)],
                                        axis=0).T)
        m_prev_all.append(mst[s])

    chains = [(s, h) for s in range(nb) for h in range(ML_HEADS)]
    grp = lambda h, part: slice(3 * LANES * h + part * LANES, 3 * LANES * h + (part + 1) * LANES)
    qs = [ml_ref[s, :, grp(h, 0)] for s, h in chains]
    ks = [ml_ref[s, :, grp(h, 1)] for s, h in chains]
    vs = [ml_ref[s, :, grp(h, 2)] for s, h in chains]
    c_old = [cst[s, h] for s, h in chains]
    s_all = [lax.dot_general(q, k, (((1,), (1,)), ((), ())), preferred_element_type=F32) for q, k in zip(qs, ks)]
    qc_all = [jnp.dot(q, c.astype(BF16), preferred_element_type=F32) for q, c in zip(qs, c_old)]

    w_all, kw_all, inter_all, mt_all, decay_all = [], [], [], [], []
    m_next = list(m_prev_all)
    for c, (s, h) in enumerate(chains):
        a_c = cols_all[s][:, h:h + 1]
        b_c = cols_all[s][:, SUBLANES + h:SUBLANES + h + 1]
        a_r = a_rows[s][h:h + 1, :]
        m_prev = m_prev_all[s][:, h:h + 1]
        g_c = jnp.maximum(cols_all[s][:, 2 * SUBLANES + h:2 * SUBLANES + h + 1], m_prev)
        d = jnp.where(causal, jnp.exp(a_r - g_c), 0.0)
        inter_all.append(jnp.exp(m_prev - g_c))
        mt_all.append(b_c + g_c)
        w_all.append((s_all[c] * d).astype(BF16))
        g_last = g_c[L - 1:L, :]
        m_new = b_c[L - 1:L, :] + g_last
        w_s = jnp.exp(a_c - g_last)
        decay_all.append(jnp.exp(m_prev - g_last))
        kw_all.append((ks[c].astype(F32) * w_s).astype(BF16))
        m_next[s] = jnp.where(lane_m == h, m_new, m_next[s])

    wv_all = [jnp.dot(w, v, preferred_element_type=F32) for w, v in zip(w_all, vs)]
    upd_all = [lax.dot_general(kw, v, (((0,), (0,)), ((), ())), preferred_element_type=F32)
               for kw, v in zip(kw_all, vs)]

    for c, (s, h) in enumerate(chains):
        nd = wv_all[c] + inter_all[c] * qc_all[c]
        num = nd[:, 0:HEAD_DIM]
        den = nd[:, HEAD_DIM:HEAD_DIM + 1]
        hh = num / jnp.maximum(jnp.abs(den), jnp.exp(-mt_all[c]))
        mu = jnp.mean(hh, axis=-1, keepdims=True)
        dh = hh - mu
        var = jnp.mean(dh * dh, axis=-1, keepdims=True)
        o = mo_ref[s, :, HEAD_DIM * h:HEAD_DIM * (h + 1)]
        gate = 1.0 / (1.0 + jnp.exp(-o))
        hn = dh * lax.rsqrt(var + LN_EPS) * nw_ref[:, HEAD_DIM * h:HEAD_DIM * (h + 1)] * gate
        h_ref[s, :, HEAD_DIM * h:HEAD_DIM * (h + 1)] = hn.astype(BF16)
        cst[s, h] = decay_all[c] * c_old[c] + upd_all[c]

    for s in range(nb):
        mst[s] = m_next[s]
        m_ref[s] = m_next[s]
    c_ref[...] = cst[...]


def _mlstm(ml, mo, g, c0, m0, nw, layer, nseq, chunk, valid_len, nb):
    m = ml.shape[0]
    t = m // nseq
    nc = t // chunk
    up = (lax.broadcasted_iota(jnp.int32, (chunk, chunk), 0)
          <= lax.broadcasted_iota(jnp.int32, (chunk, chunk), 1)).astype(BF16)
    row = lambda width: pl.BlockSpec((nb, chunk, width), lambda b, j: (b, j, 0))
    st_c = pl.BlockSpec((nb, ML_HEADS, LANES, LANES), lambda b, j: (b, 0, 0, 0))
    st_m = pl.BlockSpec((nb, 1, LANES), lambda b, j: (b, 0, 0))
    seq3 = lambda a: a.reshape(nseq, t, a.shape[-1])
    h, c_new, m_new = pl.pallas_call(
        functools.partial(_mlstm_kernel, chunk=chunk, valid_len=valid_len, nb=nb),
        grid=(nseq // nb, nc),
        in_specs=[row(ML_COLS), row(ML_WIDTH), row(LANES), st_c, st_m, _layer_spec((1, ML_WIDTH), layer),
                  _const_spec((chunk, chunk))],
        out_specs=[row(ML_WIDTH), st_c, st_m],
        out_shape=[jax.ShapeDtypeStruct((nseq, t, ML_WIDTH), BF16),
                   jax.ShapeDtypeStruct((nseq, ML_HEADS, LANES, LANES), F32),
                   jax.ShapeDtypeStruct((nseq, 1, LANES), F32)],
        scratch_shapes=[pltpu.VMEM((nb, ML_HEADS, LANES, LANES), F32), pltpu.VMEM((nb, 1, LANES), F32)],
        compiler_params=_cparams(("arbitrary", "arbitrary"), 40),
        name="mlstm",
    )(seq3(ml), seq3(mo), seq3(g), c0, m0, nw, up)
    return h.reshape(m, ML_WIDTH), c_new, m_new


def _sb_prompt_kernel(q_ref, k_ref, v_ref, tri_ref, o_ref, acc_ref, carry_ref, *, tq, tk):
    i = pl.program_id(2)
    nkb = tq // tk
    tri = tri_ref[...]
    acc_ref[...] = jnp.zeros_like(acc_ref)
    carry_ref[...] = jnp.zeros_like(carry_ref)

    def steps(items):
        chains = [(n, hh) for n in range(len(items)) for hh in range(2)]
        starts = [pl.multiple_of(j * tk, tk) for j, _, _, _ in items]
        keep = []
        for j, r0, nr, masked in items:
            if masked:
                qpos = i * tq + r0 + lax.broadcasted_iota(jnp.int32, (nr, tk), 0)
                kpos = j * tk + lax.broadcasted_iota(jnp.int32, (nr, tk), 1)
                keep.append(kpos < qpos)
            else:
                keep.append(None)
        z = [lax.dot_general(q_ref[items[n][1]:items[n][1] + items[n][2], hh * LANES:(hh + 1) * LANES],
                             k_ref[pl.ds(starts[n], tk), hh * LANES:(hh + 1) * LANES],
                             (((1,), (1,)), ((), ())), preferred_element_type=F32) for n, hh in chains]
        cum = []
        for c, (n, hh) in enumerate(chains):
            sp = _softplus2(z[c])
            if keep[n] is not None:
                sp = jnp.where(keep[n], sp, 0.0)
            cum.append(jnp.dot(sp.astype(BF16), tri, preferred_element_type=F32))
        pv = []
        for c, (n, hh) in enumerate(chains):
            a = jnp.exp2(z[c] + cum[c])
            if keep[n] is not None:
                a = jnp.where(keep[n], a, 0.0)
            pv.append(jnp.dot(a.astype(BF16), v_ref[pl.ds(starts[n], tk), :], preferred_element_type=F32))
        for c, (n, hh) in enumerate(chains):
            rows = slice(items[n][1], items[n][1] + items[n][2])
            carry = carry_ref[hh, rows]
            acc_ref[hh, rows] += jnp.exp2(carry) * pv[c]
            carry_ref[hh, rows] = carry + cum[c][:, 0:1]

    assert nkb == 2
    steps([(i * nkb + 1, tk, tk, True), (i * nkb, 0, tk, True), (i * nkb, tk, tk, False)])
    full = lambda j: (j, 0, tq, False)

    top = i * nkb - 1
    one = i % 2

    @pl.when(one == 1)
    def _():
        steps([full(top - n) for n in range(2)])

    def body(t, _):
        j0 = top - 2 * one - 4 * t
        steps([full(j0 - n) for n in range(4)])
        return 0

    lax.fori_loop(0, i // 2, body, 0)

    lane = lax.broadcasted_iota(jnp.int32, (tq, LANES), 1)
    o_ref[...] = jnp.where(lane >= HEAD_DIM, acc_ref[1], acc_ref[0]).astype(BF16)


def _sb_prompt(qa, ka, vb, nseq, tq, tk):
    m = qa.shape[0]
    t = m // nseq
    nq = t // tq
    tri = -(lax.broadcasted_iota(jnp.int32, (tk, tk), 0)
            >= lax.broadcasted_iota(jnp.int32, (tk, tk), 1)).astype(BF16)
    return pl.pallas_call(
        functools.partial(_sb_prompt_kernel, tq=tq, tk=tk),
        grid=(nseq, SB_HEADS // 2, nq),
        in_specs=[pl.BlockSpec((tq, 2 * LANES), lambda b, hp, i: (b * nq + i, hp)),
                  pl.BlockSpec((t, 2 * LANES), lambda b, hp, i: (b, hp)),
                  pl.BlockSpec((t, LANES), lambda b, hp, i: (b, hp)),
                  pl.BlockSpec((tk, tk), lambda b, hp, i: (0, 0))],
        out_specs=pl.BlockSpec((tq, LANES), lambda b, hp, i: (b * nq + i, hp)),
        out_shape=jax.ShapeDtypeStruct((m, SB_WIDTH), BF16),
        scratch_shapes=[pltpu.VMEM((2, tq, LANES), F32), pltpu.VMEM((2, tq, 1), F32)],
        compiler_params=_cparams(("arbitrary", "arbitrary", "arbitrary"), 40),
        name="sb_prompt",
    )(qa, ka, vb, tri)


def _sb_sample_kernel(pt_ref, qbd_ref, bias_ref, kn_ref, vn_ref, tri_ref, *refs, pps, nq):
    k_refs = refs[0:pps]
    v_refs = refs[pps:2 * pps]
    o_ref = refs[2 * pps]
    acc_ref, carry_ref = refs[2 * pps + 1:]
    j = pl.program_id(1)
    qbd = qbd_ref[0]
    bias = bias_ref[...]
    tri = tri_ref[...]

    def blocks(kts, vts, keep):
        z = [jnp.dot(qbd, kt[...].astype(BF16), preferred_element_type=F32) + bias for kt in kts]
        cum = []
        for zc in z:
            sp = _softplus2(zc)
            if keep is not None:
                sp = jnp.where(keep, sp, 0.0)
            cum.append(jnp.dot(sp.astype(BF16), tri, preferred_element_type=F32))
        pv = []
        for zc, cc, vt in zip(z, cum, vts):
            a = jnp.exp2(zc + cc)
            if keep is not None:
                a = jnp.where(keep, a, 0.0)
            pv.append(lax.dot_general(a.astype(BF16), vt[...].astype(BF16), (((1,), (1,)), ((), ())),
                                      preferred_element_type=F32))
        acc = acc_ref[...]
        carry = carry_ref[...]
        for pc, cc in zip(pv, cum):
            acc = acc + jnp.exp2(carry) * pc
            carry = carry + cc[:, 0:1]
        acc_ref[...] = acc
        carry_ref[...] = carry

    @pl.when(j == 0)
    def _():
        acc_ref[...] = jnp.zeros_like(acc_ref)
        carry_ref[...] = jnp.zeros_like(carry_ref)
        t_i = lax.broadcasted_iota(jnp.int32, (SB_HEADS * nq, PAGE_SIZE), 0) % nq
        s_i = lax.broadcasted_iota(jnp.int32, (SB_HEADS * nq, PAGE_SIZE), 1)
        blocks([kn_ref.at[0]], [vn_ref.at[0]], s_i < t_i)

    blocks(k_refs, v_refs, None)

    @pl.when(j == pl.num_programs(1) - 1)
    def _():
        acc = acc_ref[...]
        lane_h = lax.broadcasted_iota(jnp.int32, (nq, SB_WIDTH), 1) // HEAD_DIM
        res = jnp.zeros((nq, SB_WIDTH), F32)
        for h in range(SB_HEADS):
            res = jnp.where(lane_h == h, acc[nq * h:nq * (h + 1), :], res)
        o_ref[0] = res


def _sb_sample(qbd, bias, kt_new, vt_new, cache_kt, cache_vt, page_table, layer, nq):
    bsz, n_pages = page_table.shape
    pps = PAGES_PER_STEP
    steps = n_pages // pps
    rows = SB_HEADS * nq
    tri = -(lax.broadcasted_iota(jnp.int32, (PAGE_SIZE, PAGE_SIZE), 0)
            >= lax.broadcasted_iota(jnp.int32, (PAGE_SIZE, PAGE_SIZE), 1)).astype(BF16)

    def page_spec(p):
        def imap(b, j, pt):
            return (layer, pt[b * n_pages + (n_pages - 1 - (j * pps + p))], 0, 0)
        return pl.BlockSpec((None, None, SB_WIDTH, PAGE_SIZE), imap)

    per_seq = lambda shape: pl.BlockSpec((1,) + shape, lambda b, j, pt: (b, 0, 0))
    grid_spec = pltpu.PrefetchScalarGridSpec(
        num_scalar_prefetch=1,
        grid=(bsz, steps),
        in_specs=[per_seq((rows, SB_WIDTH)), pl.BlockSpec((rows, PAGE_SIZE), lambda b, j, pt: (0, 0)),
                  per_seq((SB_WIDTH, PAGE_SIZE)), per_seq((SB_WIDTH, PAGE_SIZE)),
                  pl.BlockSpec((PAGE_SIZE, PAGE_SIZE), lambda b, j, pt: (0, 0))]
                 + [page_spec(p) for p in range(pps)] + [page_spec(p) for p in range(pps)],
        out_specs=per_seq((nq, SB_WIDTH)),
        scratch_shapes=[pltpu.VMEM((rows, SB_WIDTH), F32), pltpu.VMEM((rows, 1), F32)],
    )
    return pl.pallas_call(
        functools.partial(_sb_sample_kernel, pps=pps, nq=nq),
        grid_spec=grid_spec,
        out_shape=jax.ShapeDtypeStruct((bsz, nq, SB_WIDTH), F32),
        compiler_params=_cparams(("arbitrary", "arbitrary"), 40),
        name="sb_sample",
    )(page_table.reshape(-1), qbd, bias, kt_new, vt_new, tri, *([cache_kt] * pps), *([cache_vt] * pps))


def _conv3(p, s_ref, cs, w_ref, tm, halo):
    s_ref[SUBLANES:SUBLANES + tm, cs] = p
    p1 = s_ref[SUBLANES - 1:SUBLANES - 1 + tm, cs]
    p2 = s_ref[SUBLANES - 2:SUBLANES - 2 + tm, cs]
    if halo is not None:
        e1, e2, seq_len = halo
        tpos = lax.broadcasted_iota(jnp.int32, p.shape, 0) % seq_len
        p1 = jnp.where(tpos == 0, e1, p1)
        p2 = jnp.where(tpos < 2, e2, p2)
    return w_ref[0:1, cs] * p2 + w_ref[1:2, cs] * p1 + w_ref[2:3, cs] * p


def _start_tile(s_ref, tiles_per_seq):
    @pl.when(pl.program_id(0) % tiles_per_seq == 0)
    def _():
        s_ref[0:SUBLANES, :] = jnp.zeros((SUBLANES, s_ref.shape[1]), F32)


def _carry_rows(s_ref, tm):
    s_ref[0:SUBLANES, :] = s_ref[tm:tm + SUBLANES, :]


def _outproj_kernel(*refs, tm, tiles_per_seq, seq_len):
    sample = seq_len < tm
    if sample:
        hml_ref, cv_ref, hsb_ref, x_ref, w_ref, cw_ref, g_ref, b_ref, e1_ref, e2_ref, y_ref, tail_ref, s_ref = refs
        halo = (e1_ref[...], e2_ref[...], seq_len)
    else:
        hml_ref, cv_ref, hsb_ref, x_ref, w_ref, cw_ref, g_ref, b_ref, y_ref, tail_ref, s_ref = refs
        halo = None
    _start_tile(s_ref, tiles_per_seq)
    cb = cv_ref[:, 0:CONV_DIM]
    p = cv_ref[:, CONV_DIM:2 * CONV_DIM] * cv_ref[:, 2 * CONV_DIM:3 * CONV_DIM]
    u = _conv3(p, s_ref, slice(None), cw_ref, tm, halo)
    if sample:
        tail_ref[...] = p
    else:
        tail_ref[...] = p[tm - SUBLANES:tm, :]
        _carry_rows(s_ref, tm)
    h_conv = (cb * u).astype(BF16)
    mix = (jnp.dot(hml_ref[...], w_ref[0:ML_WIDTH, :], preferred_element_type=F32)
           + jnp.dot(h_conv, w_ref[ML_WIDTH:ML_WIDTH + CONV_DIM, :], preferred_element_type=F32)
           + jnp.dot(hsb_ref[...].astype(BF16), w_ref[ML_WIDTH + CONV_DIM:, :], preferred_element_type=F32))
    y_ref[...] = _layer_norm(DN_ALPHA * x_ref[...] + mix, g_ref[...], b_ref[...])


def _outproj(hml, cv, hsb, x, w, cw, g, b, layer, tm, seq_len, halo=None):
    m = x.shape[0]
    sample = seq_len < tm
    tiles_per_seq = max(seq_len // tm, 1)
    row = lambda width: pl.BlockSpec((tm, width), lambda i: (i, 0))
    tail_rows = tm if sample else SUBLANES
    in_specs = [row(ML_WIDTH), row(3 * CONV_DIM), row(SB_WIDTH), row(D_MODEL),
                _layer_spec((D_MODEL, D_MODEL), layer), _layer_spec((SUBLANES, CONV_DIM), layer),
                _layer_spec((1, D_MODEL), layer), _layer_spec((1, D_MODEL), layer)]
    args = [hml, cv, hsb, x, w, cw, g, b]
    if sample:
        in_specs += [row(CONV_DIM), row(CONV_DIM)]
        args += list(halo)
    return pl.pallas_call(
        functools.partial(_outproj_kernel, tm=tm, tiles_per_seq=tiles_per_seq, seq_len=seq_len),
        grid=(m // tm,),
        in_specs=in_specs,
        out_specs=[row(D_MODEL), pl.BlockSpec((tail_rows, CONV_DIM), lambda i: (i, 0))],
        out_shape=[jax.ShapeDtypeStruct((m, D_MODEL), F32),
                   jax.ShapeDtypeStruct((m // tm * tail_rows, CONV_DIM), F32)],
        scratch_shapes=[pltpu.VMEM((tm + SUBLANES, CONV_DIM), F32)],
        compiler_params=_cparams(("arbitrary",), 40),
        name="outproj",
    )(*args)


def _ffn_kernel(*refs, tm, tiles_per_seq, seq_len):
    sample = seq_len < tm
    if sample:
        x_ref, wu_ref, cw_ref, wd_ref, g_ref, b_ref, e1_ref, e2_ref, y_ref, tail_ref, s_ref, hid_ref = refs
    else:
        x_ref, wu_ref, cw_ref, wd_ref, g_ref, b_ref, y_ref, tail_ref, s_ref, hid_ref = refs
    _start_tile(s_ref, tiles_per_seq)
    x = x_ref[...]
    xb = x.astype(BF16)
    n_chunks = D_FF // FF_CHUNK

    def up(c):
        return (jnp.dot(xb, wu_ref[:, c * FF_CHUNK:(c + 1) * FF_CHUNK], preferred_element_type=F32),
                jnp.dot(xb, wu_ref[:, D_FF + c * FF_CHUNK:D_FF + (c + 1) * FF_CHUNK], preferred_element_type=F32))

    nxt = up(0)
    for c in range(n_chunks):
        cs = slice(c * FF_CHUNK, (c + 1) * FF_CHUNK)
        g_pre, val = nxt
        if c + 1 < n_chunks:
            nxt = up(c + 1)
        halo = (e1_ref[:, cs], e2_ref[:, cs], seq_len) if sample else None
        g_conv = _conv3(g_pre, s_ref, cs, cw_ref, tm, halo)
        hid_ref[:, cs] = (g_conv / (1.0 + jnp.exp(-g_conv)) * val).astype(BF16)
    if sample:
        tail_ref[...] = s_ref[SUBLANES:SUBLANES + tm, :]
    else:
        tail_ref[...] = s_ref[tm:tm + SUBLANES, :]
        _carry_rows(s_ref, tm)
    ff = jnp.dot(hid_ref[...], wd_ref[...], preferred_element_type=F32)
    y_ref[...] = _layer_norm(DN_ALPHA * x + ff, g_ref[...], b_ref[...])


def _ffn(x, wu, cw, wd, g, b, layer, tm, seq_len, halo=None):
    m = x.shape[0]
    sample = seq_len < tm
    tiles_per_seq = max(seq_len // tm, 1)
    row = lambda width: pl.BlockSpec((tm, width), lambda i: (i, 0))
    tail_rows = tm if sample else SUBLANES
    in_specs = [row(D_MODEL),
                _layer_spec((D_MODEL, 2 * D_FF), layer, pipeline_mode=pl.Buffered(1)),
                _layer_spec((SUBLANES, D_FF), layer),
                _layer_spec((D_FF, D_MODEL), layer, pipeline_mode=pl.Buffered(1)),
                _layer_spec((1, D_MODEL), layer), _layer_spec((1, D_MODEL), layer)]
    args = [x, wu, cw, wd, g, b]
    if sample:
        in_specs += [row(D_FF), row(D_FF)]
        args += list(halo)
    return pl.pallas_call(
        functools.partial(_ffn_kernel, tm=tm, tiles_per_seq=tiles_per_seq, seq_len=seq_len),
        grid=(m // tm,),
        in_specs=in_specs,
        out_specs=[row(D_MODEL), pl.BlockSpec((tail_rows, D_FF), lambda i: (i, 0))],
        out_shape=[jax.ShapeDtypeStruct((m, D_MODEL), F32),
                   jax.ShapeDtypeStruct((m // tm * tail_rows, D_FF), F32)],
        scratch_shapes=[pltpu.VMEM((tm + SUBLANES, D_FF), F32), pltpu.VMEM((tm, D_FF), BF16)],
        compiler_params=_cparams(("arbitrary",), 56),
        name="ffn",
    )(*args)


def _split3(x):
    hi = x.astype(BF16).astype(F32)
    mid = (x - hi).astype(BF16).astype(F32)
    lo = (x - hi - mid).astype(BF16).astype(F32)
    return hi, mid, lo


def _pack_w_in(w_in, b_in, sb_bias):
    depth = w_in.shape[0]
    scale = HEAD_DIM ** -0.5
    b2 = _split3(sb_bias * LOG2E)
    zb = lambda n: jnp.zeros((depth, n), F32)
    gi = 4 * ML_WIDTH
    cv0 = gi + 2 * ML_HEADS
    sq0 = cv0 + 3 * CONV_DIM
    sk0 = sq0 + SB_WIDTH

    def head_groups(lo, nheads, factor=None):
        seg = w_in[:, :, lo:lo + HEAD_DIM * nheads].reshape(depth, D_MODEL, nheads, HEAD_DIM)
        if factor is not None:
            seg = seg * factor
        return jnp.pad(seg, ((0, 0), (0, 0), (0, 0), (0, LANES - HEAD_DIM)))

    ml = jnp.stack([head_groups(0, ML_HEADS), head_groups(ML_WIDTH, ML_HEADS, scale),
                    head_groups(2 * ML_WIDTH, ML_HEADS)], axis=3).reshape(depth, D_MODEL, ML_COLS)
    qa = head_groups(sq0, SB_HEADS, scale * LOG2E).reshape(depth, D_MODEL, SB_AUG)
    ka = head_groups(sk0, SB_HEADS).reshape(depth, D_MODEL, SB_AUG)
    gates = jnp.pad(w_in[:, :, gi:gi + 2 * ML_HEADS].reshape(depth, D_MODEL, 2, ML_HEADS),
                    ((0, 0), (0, 0), (0, 0), (0, SUBLANES - ML_HEADS))).reshape(depth, D_MODEL, 2 * SUBLANES)
    gates = jnp.pad(gates, ((0, 0), (0, 0), (0, LANES - 2 * SUBLANES)))
    w = jnp.concatenate([ml, w_in[:, :, 3 * ML_WIDTH:4 * ML_WIDTH], w_in[:, :, cv0:sq0], qa, ka, w_in[:, :, sk0:],
                         gates], axis=-1).astype(BF16)

    bp = []
    for h in range(ML_HEADS):
        lo, hi = HEAD_DIM * h, HEAD_DIM * (h + 1)
        bp += [b_in[:, lo:hi], zb(HEAD_DIM)]
        bp += [b_in[:, ML_WIDTH + lo:ML_WIDTH + hi] * scale, zb(HEAD_DIM)]
        bp += [b_in[:, 2 * ML_WIDTH + lo:2 * ML_WIDTH + hi], jnp.ones((depth, 1), F32), zb(HEAD_DIM - 1)]
    bp += [b_in[:, 3 * ML_WIDTH:4 * ML_WIDTH], b_in[:, cv0:sq0]]
    for h in range(SB_HEADS):
        lo, hi = sq0 + HEAD_DIM * h, sq0 + HEAD_DIM * (h + 1)
        bp += [b_in[:, lo:hi] * (scale * LOG2E)] + [part[:, h:h + 1] for part in b2] + [zb(HEAD_DIM - 3)]
    for h in range(SB_HEADS):
        lo, hi = sk0 + HEAD_DIM * h, sk0 + HEAD_DIM * (h + 1)
        bp += [b_in[:, lo:hi], jnp.ones((depth, 3), F32), zb(HEAD_DIM - 3)]
    bp += [b_in[:, sk0:],
           b_in[:, gi:gi + ML_HEADS], zb(SUBLANES - ML_HEADS),
           b_in[:, gi + ML_HEADS:gi + 2 * ML_HEADS], zb(LANES - SUBLANES - ML_HEADS)]
    b = jnp.concatenate(bp, axis=-1)[:, None, :]
    assert w.shape[-1] == PROJ_COLS and b.shape[-1] == PROJ_COLS
    wt = jnp.swapaxes(w_in[:, :, sk0:], 1, 2).astype(BF16)
    bt = b_in[:, sk0:, None]
    return w, b, wt, bt


def _pad_rows8(a):
    return jnp.pad(a, ((0, 0), (0, SUBLANES - a.shape[1]), (0, 0)))


def _ext_state(c, n):
    ext = jnp.concatenate([c, n[..., None]], axis=-1)
    return jnp.pad(ext, ((0, 0), (0, 0), (0, LANES - HEAD_DIM), (0, LANES - HEAD_DIM - 1)))


def _halo(prev, seq_len):
    bsz, _, c = prev.shape
    z = jnp.zeros((bsz, seq_len, c), F32)
    e1 = z.at[:, 0].set(prev[:, 1])
    e2 = z.at[:, 0].set(prev[:, 0]).at[:, 1].set(prev[:, 1])
    return e1.reshape(bsz * seq_len, c), e2.reshape(bsz * seq_len, c)


def _layer_prompt(x, nseq, wts, layer):
    (w_in, b_in, wt, bt, nw, cw, sbias, w_out, g1, b1, w_up, fcw, w_down, g2, b2) = wts
    m = x.shape[0]
    t = m // nseq
    ml, mo, cv, qa, ka, vb, skt, svt, g = _proj(x, w_in, b_in, wt, bt, layer, ROW_TILE, nseq, True)
    c0 = jnp.zeros((nseq, ML_HEADS, LANES, LANES), F32)
    m0 = jnp.zeros((nseq, 1, LANES), F32)
    hml, c_ext, m_new = _mlstm(ml, mo, g, c0, m0, nw, layer, nseq, MLSTM_CHUNK, MLSTM_CHUNK, MLSTM_SEQS)
    hsb = _sb_prompt(qa, ka, vb, nseq, ATT_TQ, ATT_TK)
    x1, ptail = _outproj(hml, cv, hsb, x, w_out, cw, g1, b1, layer, ROW_TILE, t)
    x2, gtail = _ffn(x1, w_up, fcw, w_down, g2, b2, layer, ROW_TILE, t)
    last = lambda tail: tail.reshape(nseq, t // ROW_TILE, SUBLANES, -1)[:, -1, SUBLANES - 2:, :]
    states = (skt, svt,
              c_ext[:, :, :HEAD_DIM, :HEAD_DIM], c_ext[:, :, :HEAD_DIM, HEAD_DIM], m_new[:, 0, :ML_HEADS],
              last(ptail), last(gtail))
    return x2, states


def _layer_sample(x, nseq, wts, layer, cache_kt, cache_vt, page_table, c_prev, n_prev, m_prev, conv_prev,
                  ffn_prev):
    (w_in, b_in, wt, bt, nw, cw, sbias, w_out, g1, b1, w_up, fcw, w_down, g2, b2) = wts
    m = x.shape[0]
    t = m // nseq
    ml, mo, cv, qa, _, _, sk, sv, g = _proj(x, w_in, b_in, wt, bt, layer, m, nseq, False)

    pad = lambda a: jnp.pad(a.reshape(nseq, t, -1), ((0, 0), (0, SAMPLE_PAD - t), (0, 0))).reshape(
        nseq * SAMPLE_PAD, -1)
    m0 = jnp.pad(m_prev, ((0, 0), (0, LANES - ML_HEADS)))[:, None, :]
    hml, c_ext, m_new = _mlstm(pad(ml), pad(mo), pad(g), _ext_state(c_prev, n_prev), m0, nw, layer, nseq,
                               SAMPLE_PAD, t, MLSTM_SEQS)
    hml = hml.reshape(nseq, SAMPLE_PAD, ML_WIDTH)[:, :t].reshape(m, ML_WIDTH)

    q3 = qa.reshape(nseq, t, SB_HEADS, LANES)[..., :HEAD_DIM]
    eye = jnp.eye(SB_HEADS, dtype=BF16)
    qbd = jnp.einsum("bthd,hg->bhtgd", q3, eye).reshape(nseq, SB_HEADS * t, SB_WIDTH)
    bias = jnp.broadcast_to(jnp.repeat(sbias[layer] * LOG2E, t)[:, None], (SB_HEADS * t, PAGE_SIZE))
    new_t = lambda a: jnp.pad(jnp.swapaxes(a.reshape(nseq, t, SB_WIDTH), 1, 2),
                              ((0, 0), (0, 0), (0, PAGE_SIZE - t)))
    hsb = _sb_sample(qbd, bias, new_t(sk), new_t(sv), cache_kt, cache_vt, page_table, layer, t).reshape(
        m, SB_WIDTH)

    x1, p_all = _outproj(hml, cv, hsb, x, w_out, cw, g1, b1, layer, m, t, _halo(conv_prev, t))
    x2, g_all = _ffn(x1, w_up, fcw, w_down, g2, b2, layer, m, t, _halo(ffn_prev, t))
    last = lambda a: a.reshape(nseq, t, -1)[:, t - 2:, :]
    states = (sk.reshape(nseq, t, SB_HEADS, HEAD_DIM), sv.reshape(nseq, t, SB_HEADS, HEAD_DIM),
              c_ext[:, :, :HEAD_DIM, :HEAD_DIM], c_ext[:, :, :HEAD_DIM, HEAD_DIM], m_new[:, 0, :ML_HEADS],
              last(p_all), last(g_all))
    return x2, states


def _feature_major_pages(cache):
    d, p = cache.shape[:2]
    return jnp.transpose(cache, (0, 1, 3, 4, 2)).reshape(d, p, SB_WIDTH, PAGE_SIZE)


def _token_major_state(kt):
    d, b, _, t = kt.shape
    return jnp.transpose(kt.reshape(d, b, SB_HEADS, HEAD_DIM, t), (0, 1, 4, 2, 3))


def kernel(x_prompt, x_sample, cache_k, cache_v, state_mlstm_c, state_mlstm_n, state_mlstm_m, state_conv,
           state_ffn_conv, page_table, w_in, b_in, mlstm_norm_w, conv_w, sb_bias, w_out, ln1_g, ln1_b, ffn_w_up,
           ffn_conv_w, ffn_w_down, ln2_g, ln2_b):
    bp, tp, _ = x_prompt.shape
    bs, ts, _ = x_sample.shape
    depth = w_in.shape[0]
    w_in_p, b_in_p, wt_p, bt_p = _pack_w_in(w_in, b_in, sb_bias)
    w_out_b = w_out.astype(BF16)
    w_up_b = ffn_w_up.astype(BF16)
    w_down_b = ffn_w_down.astype(BF16)
    cw_p = _pad_rows8(conv_w)
    fcw_p = _pad_rows8(ffn_conv_w)
    ckt = _feature_major_pages(cache_k)
    cvt = _feature_major_pages(cache_v)

    yp = x_prompt.reshape(bp * tp, D_MODEL)
    ys = x_sample.reshape(bs * ts, D_MODEL)
    st_p = [[] for _ in range(7)]
    st_s = [[] for _ in range(7)]
    row3 = lambda a: a[:, None, :]
    wts = (w_in_p, b_in_p, wt_p, bt_p, row3(mlstm_norm_w), cw_p, sb_bias, w_out_b, row3(ln1_g), row3(ln1_b),
           w_up_b, fcw_p, w_down_b, row3(ln2_g), row3(ln2_b))
    for l in range(depth):
        yp, new_p = _layer_prompt(yp, bp, wts, l)
        ys, new_s = _layer_sample(ys, bs, wts, l, ckt, cvt, page_table, state_mlstm_c[l], state_mlstm_n[l],
                                  state_mlstm_m[l], state_conv[l], state_ffn_conv[l])
        for lst, a in zip(st_p, new_p):
            lst.append(a)
        for lst, a in zip(st_s, new_s):
            lst.append(a)
    outs_p = [jnp.stack(s) for s in st_p]
    outs_p[0] = _token_major_state(outs_p[0])
    outs_p[1] = _token_major_state(outs_p[1])
    outs_s = [jnp.stack(s) for s in st_s]
    return (yp.reshape(bp, tp, D_MODEL), ys.reshape(bs, ts, D_MODEL), *outs_p, *outs_s)
```

```python
import functools

import jax
import jax.numpy as jnp
from jax import lax
from jax.experimental import pallas as pl
from jax.experimental.pallas import tpu as pltpu

F32 = jnp.float32
BF16 = jnp.bfloat16

D_MODEL = 1024
DEPTH = 4
ML_HEADS = 4
HEAD_DIM = 64
ML_WIDTH = ML_HEADS * HEAD_DIM
CONV_DIM = 256
CONV_WIDTH = 3
SB_HEADS = 8
SB_WIDTH = SB_HEADS * HEAD_DIM
D_FF = 2816
PAGE_SIZE = 128
LN_EPS = 1e-5
DN_ALPHA = (2 * DEPTH) ** 0.25

LANES = 128
SUBLANES = 8
MIB = 1024 * 1024

C_ML = 0
ML_COLS = ML_HEADS * 3 * LANES
C_MO = C_ML + ML_COLS
C_CV = C_MO + ML_WIDTH
SB_AUG = SB_HEADS * LANES
C_QA = C_CV + 3 * CONV_DIM
C_KA = C_QA + SB_AUG
C_SK = C_KA + SB_AUG
C_SV = C_SK + SB_WIDTH
C_G = C_SV + SB_WIDTH
PROJ_COLS = C_G + LANES
LOG2E = 1.4426950408889634

MLSTM_CHUNK = 512
MLSTM_SEQS = 1
SAMPLE_PAD = 128
ROW_TILE = 512
ATT_TQ = 512
ATT_TK = 256
PAGES_PER_STEP = 16
FF_CHUNK = 256


def _cparams(sem, vmem_mib):
    return pltpu.CompilerParams(dimension_semantics=sem, vmem_limit_bytes=vmem_mib * MIB)


def _const_spec(shape):
    zeros = (0,) * len(shape)
    return pl.BlockSpec(shape, lambda *_: zeros)


def _layer_spec(shape, layer, **kwargs):
    index = (layer,) + (0,) * len(shape)
    return pl.BlockSpec((None,) + tuple(shape), lambda *_: index, **kwargs)


def _softplus(z):
    return jnp.maximum(z, 0.0) + jnp.log(1.0 + jnp.exp(-jnp.abs(z)))


def _softplus2(z):
    sign_bit = jnp.uint32(0x80000000)
    neg_abs = lax.bitcast_convert_type(lax.bitcast_convert_type(z, jnp.uint32) | sign_bit, F32)
    return jnp.maximum(z, 0.0) + jnp.log2(1.0 + jnp.exp2(neg_abs))


def _layer_norm(y, g, b):
    mu = jnp.mean(y, axis=-1, keepdims=True)
    d = y - mu
    var = jnp.mean(d * d, axis=-1, keepdims=True)
    return d * lax.rsqrt(var + LN_EPS) * g + b


def _proj_kernel(x_ref, w_ref, b_ref, wt_ref, bt_ref, ml_ref, mo_ref, cv_ref, qa_ref, ka_ref, vb_ref, k_ref, v_ref,
                 g_ref, *, kv_transposed):
    x = x_ref[...].astype(BF16)

    def mm(c0, c1):
        return jnp.dot(x, w_ref[:, c0:c1], preferred_element_type=F32) + b_ref[:, c0:c1]

    for c in range(0, ML_COLS, 512):
        ml_ref[:, c:c + 512] = mm(C_ML + c, C_ML + c + 512).astype(BF16)
    mo_ref[...] = mm(C_MO, C_MO + ML_WIDTH)
    for c in range(0, 3 * CONV_DIM, CONV_DIM):
        cv_ref[:, c:c + CONV_DIM] = mm(C_CV + c, C_CV + c + CONV_DIM)
    for c in range(0, SB_AUG, 512):
        qa_ref[:, c:c + 512] = mm(C_QA + c, C_QA + c + 512).astype(BF16)
        ka_ref[:, c:c + 512] = mm(C_KA + c, C_KA + c + 512).astype(BF16)
    v = mm(C_SV, C_SV + SB_WIDTH)
    vb_ref[...] = v.astype(BF16)
    if kv_transposed:
        for r, ref in ((0, k_ref), (SB_WIDTH, v_ref)):
            ref[...] = lax.dot_general(wt_ref[r:r + SB_WIDTH, :], x, (((1,), (1,)), ((), ())),
                                       preferred_element_type=F32) + bt_ref[r:r + SB_WIDTH, :]
    else:
        k_ref[...] = mm(C_SK, C_SK + SB_WIDTH)
        v_ref[...] = v
    g_ref[...] = mm(C_G, C_G + LANES)


def _proj(x, w, b, wt, bt, layer, tm, nseq, kv_transposed):
    m = x.shape[0]
    t = m // nseq
    row = lambda width: pl.BlockSpec((tm, width), lambda i: (i, 0))
    outs = [(ML_COLS, BF16), (ML_WIDTH, F32), (3 * CONV_DIM, F32), (SB_AUG, BF16), (SB_AUG, BF16),
            (SB_WIDTH, BF16), (SB_WIDTH, F32), (SB_WIDTH, F32), (LANES, F32)]
    out_specs = [row(wd) for wd, _ in outs]
    out_shape = [jax.ShapeDtypeStruct((m, wd), dt) for wd, dt in outs]
    if kv_transposed:
        tiles = t // tm
        for idx in (6, 7):
            out_specs[idx] = pl.BlockSpec((None, SB_WIDTH, tm), lambda i: (i // tiles, 0, i % tiles))
            out_shape[idx] = jax.ShapeDtypeStruct((nseq, SB_WIDTH, t), F32)
    return pl.pallas_call(
        functools.partial(_proj_kernel, kv_transposed=kv_transposed),
        grid=(m // tm,),
        in_specs=[row(D_MODEL),
                  _layer_spec((D_MODEL, PROJ_COLS), layer, pipeline_mode=pl.Buffered(1)),
                  _layer_spec((1, PROJ_COLS), layer),
                  _layer_spec((2 * SB_WIDTH, D_MODEL), layer), _layer_spec((2 * SB_WIDTH, 1), layer)],
        out_specs=out_specs,
        out_shape=out_shape,
        compiler_params=_cparams(("arbitrary",), 56),
        name="proj",
    )(x, w, b, wt, bt)


def _mlstm_kernel(ml_ref, mo_ref, g_ref, c0_ref, m0_ref, nw_ref, up_ref, h_ref, c_ref, m_ref, cst, mst,
                  *, chunk, valid_len, nb):
    L = chunk

    @pl.when(pl.program_id(1) == 0)
    def _():
        cst[...] = c0_ref[...]
        mst[...] = m0_ref[...]

    up = up_ref[...]
    row_i = lax.broadcasted_iota(jnp.int32, (L, L), 0)
    col_i = lax.broadcasted_iota(jnp.int32, (L, L), 1)
    causal = row_i >= col_i
    lane_m = lax.broadcasted_iota(jnp.int32, (1, LANES), 1)

    a_rows, cols_all, m_prev_all = [], [], []
    for s in range(nb):
        gt = g_ref[s].T
        t_ig = gt[0:SUBLANES, :]
        t_f = gt[SUBLANES:2 * SUBLANES, :]
        lf = -_softplus(-t_f)
        if valid_len < L:
            pos = lax.broadcasted_iota(jnp.int32, (SUBLANES, L), 1)
            t_ig = jnp.where(pos < valid_len, t_ig, -1e30)
            lf = jnp.where(pos < valid_len, lf, 0.0)
        hi = lf.astype(BF16)
        r1 = lf - hi.astype(F32)
        mid = r1.astype(BF16)
        lo = (r1 - mid.astype(F32)).astype(BF16)
        b_row = (jnp.dot(hi, up, preferred_element_type=F32) + jnp.dot(mid, up, preferred_element_type=F32)
                 + jnp.dot(lo, up, preferred_element_type=F32))
        a_row = t_ig - b_row
        a_rows.append(a_row)
        pos_l = lax.broadcasted_iota(jnp.int32, (SUBLANES, L), 1)
        cmax = a_row
        shift = 1
        while shift < L:
            cmax = jnp.maximum(cmax, jnp.where(pos_l >= shift, pltpu.roll(cmax, shift, axis=1), -jnp.inf))
            shift *= 2
        cols_all.append(jnp.concatenate([a_row, b_row, cmax, jnp.zeros((LANES - 3 * SUBLANES, L), F32)],
                                        axis=0).T)
        m_prev_all.append(mst[s])

    chains = [(s, h) for s in range(nb) for h in range(ML_HEADS)]
    grp = lambda h, part: slice(3 * LANES * h + part * LANES, 3 * LANES * h + (part + 1) * LANES)
    qs = [ml_ref[s, :, grp(h, 0)] for s, h in chains]
    ks = [ml_ref[s, :, grp(h, 1)] for s, h in chains]
    vs = [ml_ref[s, :, grp(h, 2)] for s, h in chains]
    c_old = [cst[s, h] for s, h in chains]
    s_all = [lax.dot_general(q, k, (((1,), (1,)), ((), ())), preferred_element_type=F32) for q, k in zip(qs, ks)]
    qc_all = [jnp.dot(q, c.astype(BF16), preferred_element_type=F32) for q, c in zip(qs, c_old)]

    w_all, kw_all, inter_all, mt_all, decay_all = [], [], [], [], []
    m_next = list(m_prev_all)
    for c, (s, h) in enumerate(chains):
        a_c = cols_all[s][:, h:h + 1]
        b_c = cols_all[s][:, SUBLANES + h:SUBLANES + h + 1]
        a_r = a_rows[s][h:h + 1, :]
        m_prev = m_prev_all[s][:, h:h + 1]
        g_c = jnp.maximum(cols_all[s][:, 2 * SUBLANES + h:2 * SUBLANES + h + 1], m_prev)
        d = jnp.where(causal, jnp.exp(a_r - g_c), 0.0)
        inter_all.append(jnp.exp(m_prev - g_c))
        mt_all.append(b_c + g_c)
        w_all.append((s_all[c] * d).astype(BF16))
        g_last = g_c[L - 1:L, :]
        m_new = b_c[L - 1:L, :] + g_last
        w_s = jnp.exp(a_c - g_last)
        decay_all.append(jnp.exp(m_prev - g_last))
        kw_all.append((ks[c].astype(F32) * w_s).astype(BF16))
        m_next[s] = jnp.where(lane_m == h, m_new, m_next[s])

    wv_all = [jnp.dot(w, v, preferred_element_type=F32) for w, v in zip(w_all, vs)]
    upd_all = [lax.dot_general(kw, v, (((0,), (0,)), ((), ())), preferred_element_type=F32)
               for kw, v in zip(kw_all, vs)]

    for c, (s, h) in enumerate(chains):
        nd = wv_all[c] + inter_all[c] * qc_all[c]
        num = nd[:, 0:HEAD_DIM]
        den = nd[:, HEAD_DIM:HEAD_DIM + 1]
        hh = num / jnp.maximum(jnp.abs(den), jnp.exp(-mt_all[c]))
        mu = jnp.mean(hh, axis=-1, keepdims=True)
        dh = hh - mu
        var = jnp.mean(dh * dh, axis=-1, keepdims=True)
        o = mo_ref[s, :, HEAD_DIM * h:HEAD_DIM * (h + 1)]
        gate = 1.0 / (1.0 + jnp.exp(-o))
        hn = dh * lax.rsqrt(var + LN_EPS) * nw_ref[:, HEAD_DIM * h:HEAD_DIM * (h + 1)] * gate
        h_ref[s, :, HEAD_DIM * h:HEAD_DIM * (h + 1)] = hn.astype(BF16)
        cst[s, h] = decay_all[c] * c_old[c] + upd_all[c]

    for s in range(nb):
        mst[s] = m_next[s]
        m_ref[s] = m_next[s]
    c_ref[...] = cst[...]


def _mlstm(ml, mo, g, c0, m0, nw, layer, nseq, chunk, valid_len, nb):
    m = ml.shape[0]
    t = m // nseq
    nc = t // chunk
    up = (lax.broadcasted_iota(jnp.int32, (chunk, chunk), 0)
          <= lax.broadcasted_iota(jnp.int32, (chunk, chunk), 1)).astype(BF16)
    row = lambda width: pl.BlockSpec((nb, chunk, width), lambda b, j: (b, j, 0))
    st_c = pl.BlockSpec((nb, ML_HEADS, LANES, LANES), lambda b, j: (b, 0, 0, 0))
    st_m = pl.BlockSpec((nb, 1, LANES), lambda b, j: (b, 0, 0))
    seq3 = lambda a: a.reshape(nseq, t, a.shape[-1])
    h, c_new, m_new = pl.pallas_call(
        functools.partial(_mlstm_kernel, chunk=chunk, valid_len=valid_len, nb=nb),
        grid=(nseq // nb, nc),
        in_specs=[row(ML_COLS), row(ML_WIDTH), row(LANES), st_c, st_m, _layer_spec((1, ML_WIDTH), layer),
                  _const_spec((chunk, chunk))],
        out_specs=[row(ML_WIDTH), st_c, st_m],
        out_shape=[jax.ShapeDtypeStruct((nseq, t, ML_WIDTH), BF16),
                   jax.ShapeDtypeStruct((nseq, ML_HEADS, LANES, LANES), F32),
                   jax.ShapeDtypeStruct((nseq, 1, LANES), F32)],
        scratch_shapes=[pltpu.VMEM((nb, ML_HEADS, LANES, LANES), F32), pltpu.VMEM((nb, 1, LANES), F32)],
        compiler_params=_cparams(("arbitrary", "arbitrary"), 40),
        name="mlstm",
    )(seq3(ml), seq3(mo), seq3(g), c0, m0, nw, up)
    return h.reshape(m, ML_WIDTH), c_new, m_new


def _sb_prompt_kernel(q_ref, k_ref, v_ref, tri_ref, o_ref, acc_ref, carry_ref, *, tq, tk):
    i = pl.program_id(2)
    nkb = tq // tk
    tri = tri_ref[...]
    acc_ref[...] = jnp.zeros_like(acc_ref)
    carry_ref[...] = jnp.zeros_like(carry_ref)

    def steps(items):
        chains = [(n, hh) for n in range(len(items)) for hh in range(2)]
        starts = [pl.multiple_of(j * tk, tk) for j, _, _, _ in items]
        keep = []
        for j, r0, nr, masked in items:
            if masked:
                qpos = i * tq + r0 + lax.broadcasted_iota(jnp.int32, (nr, tk), 0)
                kpos = j * tk + lax.broadcasted_iota(jnp.int32, (nr, tk), 1)
                keep.append(kpos < qpos)
            else:
                keep.append(None)
        z = [lax.dot_general(q_ref[items[n][1]:items[n][1] + items[n][2], hh * LANES:(hh + 1) * LANES],
                             k_ref[pl.ds(starts[n], tk), hh * LANES:(hh + 1) * LANES],
                             (((1,), (1,)), ((), ())), preferred_element_type=F32) for n, hh in chains]
        cum = []
        for c, (n, hh) in enumerate(chains):
            sp = _softplus2(z[c])
            if keep[n] is not None:
                sp = jnp.where(keep[n], sp, 0.0)
            cum.append(jnp.dot(sp.astype(BF16), tri, preferred_element_type=F32))
        pv = []
        for c, (n, hh) in enumerate(chains):
            a = jnp.exp2(z[c] + cum[c])
            if keep[n] is not None:
                a = jnp.where(keep[n], a, 0.0)
            pv.append(jnp.dot(a.astype(BF16), v_ref[pl.ds(starts[n], tk), :], preferred_element_type=F32))
        for c, (n, hh) in enumerate(chains):
            rows = slice(items[n][1], items[n][1] + items[n][2])
            carry = carry_ref[hh, rows]
            acc_ref[hh, rows] += jnp.exp2(carry) * pv[c]
            carry_ref[hh, rows] = carry + cum[c][:, 0:1]

    assert nkb == 2
    steps([(i * nkb + 1, tk, tk, True), (i * nkb, 0, tk, True), (i * nkb, tk, tk, False)])
    full = lambda j: (j, 0, tq, False)

    top = i * nkb - 1
    one = i % 2

    @pl.when(one == 1)
    def _():
        steps([full(top - n) for n in range(2)])

    def body(t, _):
        j0 = top - 2 * one - 4 * t
        steps([full(j0 - n) for n in range(4)])
        return 0

    lax.fori_loop(0, i // 2, body, 0)

    lane = lax.broadcasted_iota(jnp.int32, (tq, LANES), 1)
    o_ref[...] = jnp.where(lane >= HEAD_DIM, acc_ref[1], acc_ref[0]).astype(BF16)


def _sb_prompt(qa, ka, vb, nseq, tq, tk):
    m = qa.shape[0]
    t = m // nseq
    nq = t // tq
    tri = -(lax.broadcasted_iota(jnp.int32, (tk, tk), 0)
            >= lax.broadcasted_iota(jnp.int32, (tk, tk), 1)).astype(BF16)
    return pl.pallas_call(
        functools.partial(_sb_prompt_kernel, tq=tq, tk=tk),
        grid=(nseq, SB_HEADS // 2, nq),
        in_specs=[pl.BlockSpec((tq, 2 * LANES), lambda b, hp, i: (b * nq + i, hp)),
                  pl.BlockSpec((t, 2 * LANES), lambda b, hp, i: (b, hp)),
                  pl.BlockSpec((t, LANES), lambda b, hp, i: (b, hp)),
                  pl.BlockSpec((tk, tk), lambda b, hp, i: (0, 0))],
        out_specs=pl.BlockSpec((tq, LANES), lambda b, hp, i: (b * nq + i, hp)),
        out_shape=jax.ShapeDtypeStruct((m, SB_WIDTH), BF16),
        scratch_shapes=[pltpu.VMEM((2, tq, LANES), F32), pltpu.VMEM((2, tq, 1), F32)],
        compiler_params=_cparams(("arbitrary", "arbitrary", "arbitrary"), 40),
        name="sb_prompt",
    )(qa, ka, vb, tri)


def _sb_sample_kernel(pt_ref, qbd_ref, bias_ref, kn_ref, vn_ref, tri_ref, *refs, pps, nq):
    k_refs = refs[0:pps]
    v_refs = refs[pps:2 * pps]
    o_ref = refs[2 * pps]
    acc_ref, carry_ref = refs[2 * pps + 1:]
    j = pl.program_id(1)
    qbd = qbd_ref[0]
    bias = bias_ref[...]
    tri = tri_ref[...]

    def blocks(kts, vts, keep):
        z = [jnp.dot(qbd, kt[...].astype(BF16), preferred_element_type=F32) + bias for kt in kts]
        cum = []
        for zc in z:
            sp = _softplus2(zc)
            if keep is not None:
                sp = jnp.where(keep, sp, 0.0)
            cum.append(jnp.dot(sp.astype(BF16), tri, preferred_element_type=F32))
        pv = []
        for zc, cc, vt in zip(z, cum, vts):
            a = jnp.exp2(zc + cc)
            if keep is not None:
                a = jnp.where(keep, a, 0.0)
            pv.append(lax.dot_general(a.astype(BF16), vt[...].astype(BF16), (((1,), (1,)), ((), ())),
                                      preferred_element_type=F32))
        acc = acc_ref[...]
        carry = carry_ref[...]
        for pc, cc in zip(pv, cum):
            acc = acc + jnp.exp2(carry) * pc
            carry = carry + cc[:, 0:1]
        acc_ref[...] = acc
        carry_ref[...] = carry

    @pl.when(j == 0)
    def _():
        acc_ref[...] = jnp.zeros_like(acc_ref)
        carry_ref[...] = jnp.zeros_like(carry_ref)
        t_i = lax.broadcasted_iota(jnp.int32, (SB_HEADS * nq, PAGE_SIZE), 0) % nq
        s_i = lax.broadcasted_iota(jnp.int32, (SB_HEADS * nq, PAGE_SIZE), 1)
        blocks([kn_ref.at[0]], [vn_ref.at[0]], s_i < t_i)

    blocks(k_refs, v_refs, None)

    @pl.when(j == pl.num_programs(1) - 1)
    def _():
        acc = acc_ref[...]
        lane_h = lax.broadcasted_iota(jnp.int32, (nq, SB_WIDTH), 1) // HEAD_DIM
        res = jnp.zeros((nq, SB_WIDTH), F32)
        for h in range(SB_HEADS):
            res = jnp.where(lane_h == h, acc[nq * h:nq * (h + 1), :], res)
        o_ref[0] = res


def _sb_sample(qbd, bias, kt_new, vt_new, cache_kt, cache_vt, page_table, layer, nq):
    bsz, n_pages = page_table.shape
    pps = PAGES_PER_STEP
    steps = n_pages // pps
    rows = SB_HEADS * nq
    tri = -(lax.broadcasted_iota(jnp.int32, (PAGE_SIZE, PAGE_SIZE), 0)
            >= lax.broadcasted_iota(jnp.int32, (PAGE_SIZE, PAGE_SIZE), 1)).astype(BF16)

    def page_spec(p):
        def imap(b, j, pt):
            return (layer, pt[b * n_pages + (n_pages - 1 - (j * pps + p))], 0, 0)
        return pl.BlockSpec((None, None, SB_WIDTH, PAGE_SIZE), imap)

    per_seq = lambda shape: pl.BlockSpec((1,) + shape, lambda b, j, pt: (b, 0, 0))
    grid_spec = pltpu.PrefetchScalarGridSpec(
        num_scalar_prefetch=1,
        grid=(bsz, steps),
        in_specs=[per_seq((rows, SB_WIDTH)), pl.BlockSpec((rows, PAGE_SIZE), lambda b, j, pt: (0, 0)),
                  per_seq((SB_WIDTH, PAGE_SIZE)), per_seq((SB_WIDTH, PAGE_SIZE)),
                  pl.BlockSpec((PAGE_SIZE, PAGE_SIZE), lambda b, j, pt: (0, 0))]
                 + [page_spec(p) for p in range(pps)] + [page_spec(p) for p in range(pps)],
        out_specs=per_seq((nq, SB_WIDTH)),
        scratch_shapes=[pltpu.VMEM((rows, SB_WIDTH), F32), pltpu.VMEM((rows, 1), F32)],
    )
    return pl.pallas_call(
        functools.partial(_sb_sample_kernel, pps=pps, nq=nq),
        grid_spec=grid_spec,
        out_shape=jax.ShapeDtypeStruct((bsz, nq, SB_WIDTH), F32),
        compiler_params=_cparams(("arbitrary", "arbitrary"), 40),
        name="sb_sample",
    )(page_table.reshape(-1), qbd, bias, kt_new, vt_new, tri, *([cache_kt] * pps), *([cache_vt] * pps))


def _conv3(p, s_ref, cs, w_ref, tm, halo):
    s_ref[SUBLANES:SUBLANES + tm, cs] = p
    p1 = s_ref[SUBLANES - 1:SUBLANES - 1 + tm, cs]
    p2 = s_ref[SUBLANES - 2:SUBLANES - 2 + tm, cs]
    if halo is not None:
        e1, e2, seq_len = halo
        tpos = lax.broadcasted_iota(jnp.int32, p.shape, 0) % seq_len
        p1 = jnp.where(tpos == 0, e1, p1)
        p2 = jnp.where(tpos < 2, e2, p2)
    return w_ref[0:1, cs] * p2 + w_ref[1:2, cs] * p1 + w_ref[2:3, cs] * p


def _start_tile(s_ref, tiles_per_seq):
    @pl.when(pl.program_id(0) % tiles_per_seq == 0)
    def _():
        s_ref[0:SUBLANES, :] = jnp.zeros((SUBLANES, s_ref.shape[1]), F32)


def _carry_rows(s_ref, tm):
    s_ref[0:SUBLANES, :] = s_ref[tm:tm + SUBLANES, :]


def _mix_rows(hml_ref, cv_ref, hsb_ref, x_ref, w_ref, cw_ref, g_ref, b_ref, tail_ref, s_ref, halo_refs, tm, seq_len):
    sample = halo_refs is not None
    halo = (halo_refs[0][...], halo_refs[1][...], seq_len) if sample else None
    cb = cv_ref[:, 0:CONV_DIM]
    p = cv_ref[:, CONV_DIM:2 * CONV_DIM] * cv_ref[:, 2 * CONV_DIM:3 * CONV_DIM]
    u = _conv3(p, s_ref, slice(None), cw_ref, tm, halo)
    if sample:
        tail_ref[...] = p
    else:
        tail_ref[...] = p[tm - SUBLANES:tm, :]
        _carry_rows(s_ref, tm)
    h_conv = (cb * u).astype(BF16)
    mix = (jnp.dot(hml_ref[...], w_ref[0:ML_WIDTH, :], preferred_element_type=F32)
           + jnp.dot(h_conv, w_ref[ML_WIDTH:ML_WIDTH + CONV_DIM, :], preferred_element_type=F32)
           + jnp.dot(hsb_ref[...].astype(BF16), w_ref[ML_WIDTH + CONV_DIM:, :], preferred_element_type=F32))
    return _layer_norm(DN_ALPHA * x_ref[...] + mix, g_ref[...], b_ref[...])


def _ffn_rows(x, wu_ref, cw_ref, wd_ref, g_ref, b_ref, tail_ref, s_ref, hid_ref, halo_refs, tm, seq_len):
    sample = halo_refs is not None
    if sample:
        e1_ref, e2_ref = halo_refs
    xb = x.astype(BF16)
    n_chunks = D_FF // FF_CHUNK

    def up(c):
        return (jnp.dot(xb, wu_ref[:, c * FF_CHUNK:(c + 1) * FF_CHUNK], preferred_element_type=F32),
                jnp.dot(xb, wu_ref[:, D_FF + c * FF_CHUNK:D_FF + (c + 1) * FF_CHUNK], preferred_element_type=F32))

    nxt = up(0)
    for c in range(n_chunks):
        cs = slice(c * FF_CHUNK, (c + 1) * FF_CHUNK)
        g_pre, val = nxt
        if c + 1 < n_chunks:
            nxt = up(c + 1)
        halo = (e1_ref[:, cs], e2_ref[:, cs], seq_len) if sample else None
        g_conv = _conv3(g_pre, s_ref, cs, cw_ref, tm, halo)
        hid_ref[:, cs] = (g_conv / (1.0 + jnp.exp(-g_conv)) * val).astype(BF16)
    if sample:
        tail_ref[...] = s_ref[SUBLANES:SUBLANES + tm, :]
    else:
        tail_ref[...] = s_ref[tm:tm + SUBLANES, :]
        _carry_rows(s_ref, tm)
    ff = jnp.dot(hid_ref[...], wd_ref[...], preferred_element_type=F32)
    return _layer_norm(DN_ALPHA * x + ff, g_ref[...], b_ref[...])


def _mix_ffn_kernel(*refs, tm, tiles_per_seq, seq_len):
    sample = seq_len < tm
    (hml_ref, cv_ref, hsb_ref, x_ref, wo_ref, cw_ref, g1_ref, b1_ref, wu_ref, fcw_ref, wd_ref, g2_ref,
     b2_ref) = refs[:13]
    if sample:
        e1c_ref, e2c_ref, e1f_ref, e2f_ref, y_ref, ctail_ref, ftail_ref, sc_ref, sf_ref, hid_ref = refs[13:]
        halo_c, halo_f = (e1c_ref, e2c_ref), (e1f_ref, e2f_ref)
    else:
        y_ref, ctail_ref, ftail_ref, sc_ref, sf_ref, hid_ref = refs[13:]
        halo_c = halo_f = None
    _start_tile(sc_ref, tiles_per_seq)
    _start_tile(sf_ref, tiles_per_seq)
    x1 = _mix_rows(hml_ref, cv_ref, hsb_ref, x_ref, wo_ref, cw_ref, g1_ref, b1_ref, ctail_ref, sc_ref, halo_c, tm,
                   seq_len)
    y_ref[...] = _ffn_rows(x1, wu_ref, fcw_ref, wd_ref, g2_ref, b2_ref, ftail_ref, sf_ref, hid_ref, halo_f, tm,
                           seq_len)


def _mix_ffn(hml, cv, hsb, x, wo, cw, g1, b1, wu, fcw, wd, g2, b2, layer, tm, seq_len, halo_c=None, halo_f=None):
    m = x.shape[0]
    sample = seq_len < tm
    tiles_per_seq = max(seq_len // tm, 1)
    row = lambda width: pl.BlockSpec((tm, width), lambda i: (i, 0))
    tail_rows = tm if sample else SUBLANES
    tail = lambda width: pl.BlockSpec((tail_rows, width), lambda i: (i, 0))
    in_specs = [row(ML_WIDTH), row(3 * CONV_DIM), row(SB_WIDTH), row(D_MODEL),
                _layer_spec((D_MODEL, D_MODEL), layer, pipeline_mode=pl.Buffered(1)),
                _layer_spec((SUBLANES, CONV_DIM), layer),
                _layer_spec((1, D_MODEL), layer), _layer_spec((1, D_MODEL), layer),
                _layer_spec((D_MODEL, 2 * D_FF), layer, pipeline_mode=pl.Buffered(1)),
                _layer_spec((SUBLANES, D_FF), layer),
                _layer_spec((D_FF, D_MODEL), layer, pipeline_mode=pl.Buffered(1)),
                _layer_spec((1, D_MODEL), layer), _layer_spec((1, D_MODEL), layer)]
    args = [hml, cv, hsb, x, wo, cw, g1, b1, wu, fcw, wd, g2, b2]
    if sample:
        in_specs += [row(CONV_DIM), row(CONV_DIM), row(D_FF), row(D_FF)]
        args += list(halo_c) + list(halo_f)
    return pl.pallas_call(
        functools.partial(_mix_ffn_kernel, tm=tm, tiles_per_seq=tiles_per_seq, seq_len=seq_len),
        grid=(m // tm,),
        in_specs=in_specs,
        out_specs=[row(D_MODEL), tail(CONV_DIM), tail(D_FF)],
        out_shape=[jax.ShapeDtypeStruct((m, D_MODEL), F32),
                   jax.ShapeDtypeStruct((m // tm * tail_rows, CONV_DIM), F32),
                   jax.ShapeDtypeStruct((m // tm * tail_rows, D_FF), F32)],
        scratch_shapes=[pltpu.VMEM((tm + SUBLANES, CONV_DIM), F32), pltpu.VMEM((tm + SUBLANES, D_FF), F32),
                        pltpu.VMEM((tm, D_FF), BF16)],
        compiler_params=_cparams(("arbitrary",), 56),
        name="mix_ffn",
    )(*args)


def _split3(x):
    hi = x.astype(BF16).astype(F32)
    mid = (x - hi).astype(BF16).astype(F32)
    lo = (x - hi - mid).astype(BF16).astype(F32)
    return hi, mid, lo


def _pack_w_in(w_in, b_in, sb_bias):
    depth = w_in.shape[0]
    scale = HEAD_DIM ** -0.5
    b2 = _split3(sb_bias * LOG2E)
    zb = lambda n: jnp.zeros((depth, n), F32)
    gi = 4 * ML_WIDTH
    cv0 = gi + 2 * ML_HEADS
    sq0 = cv0 + 3 * CONV_DIM
    sk0 = sq0 + SB_WIDTH

    def head_groups(lo, nheads, factor=None):
        seg = w_in[:, :, lo:lo + HEAD_DIM * nheads].reshape(depth, D_MODEL, nheads, HEAD_DIM)
        if factor is not None:
            seg = seg * factor
        return jnp.pad(seg, ((0, 0), (0, 0), (0, 0), (0, LANES - HEAD_DIM)))

    ml = jnp.stack([head_groups(0, ML_HEADS), head_groups(ML_WIDTH, ML_HEADS, scale),
                    head_groups(2 * ML_WIDTH, ML_HEADS)], axis=3).reshape(depth, D_MODEL, ML_COLS)
    qa = head_groups(sq0, SB_HEADS, scale * LOG2E).reshape(depth, D_MODEL, SB_AUG)
    ka = head_groups(sk0, SB_HEADS).reshape(depth, D_MODEL, SB_AUG)
    gates = jnp.pad(w_in[:, :, gi:gi + 2 * ML_HEADS].reshape(depth, D_MODEL, 2, ML_HEADS),
                    ((0, 0), (0, 0), (0, 0), (0, SUBLANES - ML_HEADS))).reshape(depth, D_MODEL, 2 * SUBLANES)
    gates = jnp.pad(gates, ((0, 0), (0, 0), (0, LANES - 2 * SUBLANES)))
    w = jnp.concatenate([ml, w_in[:, :, 3 * ML_WIDTH:4 * ML_WIDTH], w_in[:, :, cv0:sq0], qa, ka, w_in[:, :, sk0:],
                         gates], axis=-1).astype(BF16)

    bp = []
    for h in range(ML_HEADS):
        lo, hi = HEAD_DIM * h, HEAD_DIM * (h + 1)
        bp += [b_in[:, lo:hi], zb(HEAD_DIM)]
        bp += [b_in[:, ML_WIDTH + lo:ML_WIDTH + hi] * scale, zb(HEAD_DIM)]
        bp += [b_in[:, 2 * ML_WIDTH + lo:2 * ML_WIDTH + hi], jnp.ones((depth, 1), F32), zb(HEAD_DIM - 1)]
    bp += [b_in[:, 3 * ML_WIDTH:4 * ML_WIDTH], b_in[:, cv0:sq0]]
    for h in range(SB_HEADS):
        lo, hi = sq0 + HEAD_DIM * h, sq0 + HEAD_DIM * (h + 1)
        bp += [b_in[:, lo:hi] * (scale * LOG2E)] + [part[:, h:h + 1] for part in b2] + [zb(HEAD_DIM - 3)]
    for h in range(SB_HEADS):
        lo, hi = sk0 + HEAD_DIM * h, sk0 + HEAD_DIM * (h + 1)
        bp += [b_in[:, lo:hi], jnp.ones((depth, 3), F32), zb(HEAD_DIM - 3)]
    bp += [b_in[:, sk0:],
           b_in[:, gi:gi + ML_HEADS], zb(SUBLANES - ML_HEADS),
           b_in[:, gi + ML_HEADS:gi + 2 * ML_HEADS], zb(LANES - SUBLANES - ML_HEADS)]
    b = jnp.concatenate(bp, axis=-1)[:, None, :]
    assert w.shape[-1] == PROJ_COLS and b.shape[-1] == PROJ_COLS
    wt = jnp.swapaxes(w_in[:, :, sk0:], 1, 2).astype(BF16)
    bt = b_in[:, sk0:, None]
    return w, b, wt, bt


def _pad_rows8(a):
    return jnp.pad(a, ((0, 0), (0, SUBLANES - a.shape[1]), (0, 0)))


def _ext_state(c, n):
    ext = jnp.concatenate([c, n[..., None]], axis=-1)
    return jnp.pad(ext, ((0, 0), (0, 0), (0, LANES - HEAD_DIM), (0, LANES - HEAD_DIM - 1)))


def _halo(prev, seq_len):
    bsz, _, c = prev.shape
    z = jnp.zeros((bsz, seq_len, c), F32)
    e1 = z.at[:, 0].set(prev[:, 1])
    e2 = z.at[:, 0].set(prev[:, 0]).at[:, 1].set(prev[:, 1])
    return e1.reshape(bsz * seq_len, c), e2.reshape(bsz * seq_len, c)


def _layer_prompt(x, nseq, wts, layer):
    (w_in, b_in, wt, bt, nw, cw, sbias, w_out, g1, b1, w_up, fcw, w_down, g2, b2) = wts
    m = x.shape[0]
    t = m // nseq
    ml, mo, cv, qa, ka, vb, skt, svt, g = _proj(x, w_in, b_in, wt, bt, layer, ROW_TILE, nseq, True)
    c0 = jnp.zeros((nseq, ML_HEADS, LANES, LANES), F32)
    m0 = jnp.zeros((nseq, 1, LANES), F32)
    hml, c_ext, m_new = _mlstm(ml, mo, g, c0, m0, nw, layer, nseq, MLSTM_CHUNK, MLSTM_CHUNK, MLSTM_SEQS)
    hsb = _sb_prompt(qa, ka, vb, nseq, ATT_TQ, ATT_TK)
    x2, ptail, gtail = _mix_ffn(hml, cv, hsb, x, w_out, cw, g1, b1, w_up, fcw, w_down, g2, b2, layer, ROW_TILE, t)
    last = lambda tail: tail.reshape(nseq, t // ROW_TILE, SUBLANES, -1)[:, -1, SUBLANES - 2:, :]
    states = (skt, svt,
              c_ext[:, :, :HEAD_DIM, :HEAD_DIM], c_ext[:, :, :HEAD_DIM, HEAD_DIM], m_new[:, 0, :ML_HEADS],
              last(ptail), last(gtail))
    return x2, states


def _layer_sample(x, nseq, wts, layer, cache_kt, cache_vt, page_table, c_prev, n_prev, m_prev, conv_prev,
                  ffn_prev):
    (w_in, b_in, wt, bt, nw, cw, sbias, w_out, g1, b1, w_up, fcw, w_down, g2, b2) = wts
    m = x.shape[0]
    t = m // nseq
    ml, mo, cv, qa, _, _, sk, sv, g = _proj(x, w_in, b_in, wt, bt, layer, m, nseq, False)

    pad = lambda a: jnp.pad(a.reshape(nseq, t, -1), ((0, 0), (0, SAMPLE_PAD - t), (0, 0))).reshape(
        nseq * SAMPLE_PAD, -1)
    m0 = jnp.pad(m_prev, ((0, 0), (0, LANES - ML_HEADS)))[:, None, :]
    hml, c_ext, m_new = _mlstm(pad(ml), pad(mo), pad(g), _ext_state(c_prev, n_prev), m0, nw, layer, nseq,
                               SAMPLE_PAD, t, MLSTM_SEQS)
    hml = hml.reshape(nseq, SAMPLE_PAD, ML_WIDTH)[:, :t].reshape(m, ML_WIDTH)

    q3 = qa.reshape(nseq, t, SB_HEADS, LANES)[..., :HEAD_DIM]
    eye = jnp.eye(SB_HEADS, dtype=BF16)
    qbd = jnp.einsum("bthd,hg->bhtgd", q3, eye).reshape(nseq, SB_HEADS * t, SB_WIDTH)
    bias = jnp.broadcast_to(jnp.repeat(sbias[layer] * LOG2E, t)[:, None], (SB_HEADS * t, PAGE_SIZE))
    new_t = lambda a: jnp.pad(jnp.swapaxes(a.reshape(nseq, t, SB_WIDTH), 1, 2),
                              ((0, 0), (0, 0), (0, PAGE_SIZE - t)))
    hsb = _sb_sample(qbd, bias, new_t(sk), new_t(sv), cache_kt, cache_vt, page_table, layer, t).reshape(
        m, SB_WIDTH)

    x2, p_all, g_all = _mix_ffn(hml, cv, hsb, x, w_out, cw, g1, b1, w_up, fcw, w_down, g2, b2, layer, m, t,
                                _halo(conv_prev, t), _halo(ffn_prev, t))
    last = lambda a: a.reshape(nseq, t, -1)[:, t - 2:, :]
    states = (sk.reshape(nseq, t, SB_HEADS, HEAD_DIM), sv.reshape(nseq, t, SB_HEADS, HEAD_DIM),
              c_ext[:, :, :HEAD_DIM, :HEAD_DIM], c_ext[:, :, :HEAD_DIM, HEAD_DIM], m_new[:, 0, :ML_HEADS],
              last(p_all), last(g_all))
    return x2, states


def _feature_major_pages(cache):
    d, p = cache.shape[:2]
    return jnp.transpose(cache, (0, 1, 3, 4, 2)).reshape(d, p, SB_WIDTH, PAGE_SIZE)


def _token_major_state(kt):
    d, b, _, t = kt.shape
    return jnp.transpose(kt.reshape(d, b, SB_HEADS, HEAD_DIM, t), (0, 1, 4, 2, 3))


def kernel(x_prompt, x_sample, cache_k, cache_v, state_mlstm_c, state_mlstm_n, state_mlstm_m, state_conv,
           state_ffn_conv, page_table, w_in, b_in, mlstm_norm_w, conv_w, sb_bias, w_out, ln1_g, ln1_b, ffn_w_up,
           ffn_conv_w, ffn_w_down, ln2_g, ln2_b):
    bp, tp, _ = x_prompt.shape
    bs, ts, _ = x_sample.shape
    depth = w_in.shape[0]
    w_in_p, b_in_p, wt_p, bt_p = _pack_w_in(w_in, b_in, sb_bias)
    w_out_b = w_out.astype(BF16)
    w_up_b = ffn_w_up.astype(BF16)
    w_down_b = ffn_w_down.astype(BF16)
    cw_p = _pad_rows8(conv_w)
    fcw_p = _pad_rows8(ffn_conv_w)
    ckt = _feature_major_pages(cache_k)
    cvt = _feature_major_pages(cache_v)

    yp = x_prompt.reshape(bp * tp, D_MODEL)
    ys = x_sample.reshape(bs * ts, D_MODEL)
    st_p = [[] for _ in range(7)]
    st_s = [[] for _ in range(7)]
    row3 = lambda a: a[:, None, :]
    wts = (w_in_p, b_in_p, wt_p, bt_p, row3(mlstm_norm_w), cw_p, sb_bias, w_out_b, row3(ln1_g), row3(ln1_b),
           w_up_b, fcw_p, w_down_b, row3(ln2_g), row3(ln2_b))
    for l in range(depth):
        yp, new_p = _layer_prompt(yp, bp, wts, l)
        ys, new_s = _layer_sample(ys, bs, wts, l, ckt, cvt, page_table, state_mlstm_c[l], state_mlstm_n[l],
                                  state_mlstm_m[l], state_conv[l], state_ffn_conv[l])
        for lst, a in zip(st_p, new_p):
            lst.append(a)
        for lst, a in zip(st_s, new_s):
            lst.append(a)
    outs_p = [jnp.stack(s) for s in st_p]
    outs_p[0] = _token_major_state(outs_p[0])
    outs_p[1] = _token_major_state(outs_p[1])
    outs_s = [jnp.stack(s) for s in st_s]
    return (yp.reshape(bp, tp, D_MODEL), ys.reshape(bs, ts, D_MODEL), *outs_p, *outs_s)
```

```python
import functools

import jax
import jax.numpy as jnp
from jax import lax
from jax.experimental import pallas as pl
from jax.experimental.pallas import tpu as pltpu

F32 = jnp.float32
BF16 = jnp.bfloat16

D_MODEL = 1024
DEPTH = 4
ML_HEADS = 4
HEAD_DIM = 64
ML_WIDTH = ML_HEADS * HEAD_DIM
CONV_DIM = 256
CONV_WIDTH = 3
SB_HEADS = 8
SB_WIDTH = SB_HEADS * HEAD_DIM
D_FF = 2816
PAGE_SIZE = 128
LN_EPS = 1e-5
DN_ALPHA = (2 * DEPTH) ** 0.25

LANES = 128
SUBLANES = 8
MIB = 1024 * 1024

C_ML = 0
ML_COLS = ML_HEADS * 3 * LANES
C_MO = C_ML + ML_COLS
C_CV = C_MO + ML_WIDTH
C_SQ = C_CV + 3 * CONV_DIM
C_SK = C_SQ + SB_WIDTH
C_SV = C_SK + SB_WIDTH
C_G = C_SV + SB_WIDTH
PROJ_COLS = C_G + LANES
LOG2E = 1.4426950408889634
BIAS_LANES = 3

MLSTM_CHUNK = 512
MLSTM_SEQS = 1
SAMPLE_PAD = 128
ROW_TILE = 512
ATT_TQ = 512
ATT_TK = 256
PAGES_PER_STEP = 16
FF_CHUNK = 256


def _cparams(sem, vmem_mib):
    return pltpu.CompilerParams(dimension_semantics=sem, vmem_limit_bytes=vmem_mib * MIB)


def _const_spec(shape):
    zeros = (0,) * len(shape)
    return pl.BlockSpec(shape, lambda *_: zeros)


def _layer_spec(shape, layer, **kwargs):
    index = (layer,) + (0,) * len(shape)
    return pl.BlockSpec((None,) + tuple(shape), lambda *_: index, **kwargs)


def _softplus(z):
    return jnp.maximum(z, 0.0) + jnp.log(1.0 + jnp.exp(-jnp.abs(z)))


def _softplus2(z):
    sign_bit = jnp.uint32(0x80000000)
    neg_abs = lax.bitcast_convert_type(lax.bitcast_convert_type(z, jnp.uint32) | sign_bit, F32)
    return jnp.maximum(z, 0.0) + jnp.log2(1.0 + jnp.exp2(neg_abs))


def _layer_norm(y, g, b):
    mu = jnp.mean(y, axis=-1, keepdims=True)
    d = y - mu
    var = jnp.mean(d * d, axis=-1, keepdims=True)
    return d * lax.rsqrt(var + LN_EPS) * g + b


def _proj_kernel(x_ref, w_ref, b_ref, wt_ref, bt_ref, ml_ref, mo_ref, cv_ref, qb_ref, kb_ref, vb_ref, k_ref, v_ref,
                 g_ref, *, kv_transposed):
    x = x_ref[...].astype(BF16)

    def mm(c0, c1):
        return jnp.dot(x, w_ref[:, c0:c1], preferred_element_type=F32) + b_ref[:, c0:c1]

    for c in range(0, ML_COLS, 512):
        ml_ref[:, c:c + 512] = mm(C_ML + c, C_ML + c + 512).astype(BF16)
    mo_ref[...] = mm(C_MO, C_MO + ML_WIDTH)
    for c in range(0, 3 * CONV_DIM, CONV_DIM):
        cv_ref[:, c:c + CONV_DIM] = mm(C_CV + c, C_CV + c + CONV_DIM)
    qb_ref[...] = mm(C_SQ, C_SQ + SB_WIDTH).astype(BF16)
    k = mm(C_SK, C_SK + SB_WIDTH)
    kb_ref[...] = k.astype(BF16)
    v = mm(C_SV, C_SV + SB_WIDTH)
    vb_ref[...] = v.astype(BF16)
    if kv_transposed:
        for r, ref in ((0, k_ref), (SB_WIDTH, v_ref)):
            ref[...] = lax.dot_general(wt_ref[r:r + SB_WIDTH, :], x, (((1,), (1,)), ((), ())),
                                       preferred_element_type=F32) + bt_ref[r:r + SB_WIDTH, :]
    else:
        k_ref[...] = k
        v_ref[...] = v
    g_ref[...] = mm(C_G, C_G + LANES)


def _proj(x, w, b, wt, bt, layer, tm, nseq, kv_transposed):
    m = x.shape[0]
    t = m // nseq
    row = lambda width: pl.BlockSpec((tm, width), lambda i: (i, 0))
    outs = [(ML_COLS, BF16), (ML_WIDTH, F32), (3 * CONV_DIM, F32), (SB_WIDTH, BF16), (SB_WIDTH, BF16),
            (SB_WIDTH, BF16), (SB_WIDTH, F32), (SB_WIDTH, F32), (LANES, F32)]
    out_specs = [row(wd) for wd, _ in outs]
    out_shape = [jax.ShapeDtypeStruct((m, wd), dt) for wd, dt in outs]
    if kv_transposed:
        tiles = t // tm
        for idx in (6, 7):
            out_specs[idx] = pl.BlockSpec((None, SB_WIDTH, tm), lambda i: (i // tiles, 0, i % tiles))
            out_shape[idx] = jax.ShapeDtypeStruct((nseq, SB_WIDTH, t), F32)
    return pl.pallas_call(
        functools.partial(_proj_kernel, kv_transposed=kv_transposed),
        grid=(m // tm,),
        in_specs=[row(D_MODEL),
                  _layer_spec((D_MODEL, PROJ_COLS), layer, pipeline_mode=pl.Buffered(1)),
                  _layer_spec((1, PROJ_COLS), layer),
                  _layer_spec((2 * SB_WIDTH, D_MODEL), layer), _layer_spec((2 * SB_WIDTH, 1), layer)],
        out_specs=out_specs,
        out_shape=out_shape,
        compiler_params=_cparams(("arbitrary",), 56),
        name="proj",
    )(x, w, b, wt, bt)


def _mlstm_kernel(ml_ref, mo_ref, g_ref, c0_ref, m0_ref, nw_ref, up_ref, h_ref, c_ref, m_ref, cst, mst,
                  *, chunk, valid_len, nb):
    L = chunk

    @pl.when(pl.program_id(1) == 0)
    def _():
        cst[...] = c0_ref[...]
        mst[...] = m0_ref[...]

    up = up_ref[...]
    row_i = lax.broadcasted_iota(jnp.int32, (L, L), 0)
    col_i = lax.broadcasted_iota(jnp.int32, (L, L), 1)
    causal = row_i >= col_i
    lane_m = lax.broadcasted_iota(jnp.int32, (1, LANES), 1)

    a_rows, cols_all, m_prev_all = [], [], []
    for s in range(nb):
        gt = g_ref[s].T
        t_ig = gt[0:SUBLANES, :]
        t_f = gt[SUBLANES:2 * SUBLANES, :]
        lf = -_softplus(-t_f)
        if valid_len < L:
            pos = lax.broadcasted_iota(jnp.int32, (SUBLANES, L), 1)
            t_ig = jnp.where(pos < valid_len, t_ig, -1e30)
            lf = jnp.where(pos < valid_len, lf, 0.0)
        hi = lf.astype(BF16)
        r1 = lf - hi.astype(F32)
        mid = r1.astype(BF16)
        lo = (r1 - mid.astype(F32)).astype(BF16)
        b_row = (jnp.dot(hi, up, preferred_element_type=F32) + jnp.dot(mid, up, preferred_element_type=F32)
                 + jnp.dot(lo, up, preferred_element_type=F32))
        a_row = t_ig - b_row
        a_rows.append(a_row)
        pos_l = lax.broadcasted_iota(jnp.int32, (SUBLANES, L), 1)
        cmax = a_row
        shift = 1
        while shift < L:
            cmax = jnp.maximum(cmax, jnp.where(pos_l >= shift, pltpu.roll(cmax, shift, axis=1), -jnp.inf))
            shift *= 2
        cols_all.append(jnp.concatenate([a_row, b_row, cmax, jnp.zeros((LANES - 3 * SUBLANES, L), F32)],
                                        axis=0).T)
        m_prev_all.append(mst[s])

    chains = [(s, h) for s in range(nb) for h in range(ML_HEADS)]
    grp = lambda h, part: slice(3 * LANES * h + part * LANES, 3 * LANES * h + (part + 1) * LANES)
    qs = [ml_ref[s, :, grp(h, 0)] for s, h in chains]
    ks = [ml_ref[s, :, grp(h, 1)] for s, h in chains]
    vs = [ml_ref[s, :, grp(h, 2)] for s, h in chains]
    c_old = [cst[s, h] for s, h in chains]
    s_all = [lax.dot_general(q, k, (((1,), (1,)), ((), ())), preferred_element_type=F32) for q, k in zip(qs, ks)]
    qc_all = [jnp.dot(q, c.astype(BF16), preferred_element_type=F32) for q, c in zip(qs, c_old)]

    w_all, kw_all, inter_all, mt_all, decay_all = [], [], [], [], []
    m_next = list(m_prev_all)
    for c, (s, h) in enumerate(chains):
        a_c = cols_all[s][:, h:h + 1]
        b_c = cols_all[s][:, SUBLANES + h:SUBLANES + h + 1]
        a_r = a_rows[s][h:h + 1, :]
        m_prev = m_prev_all[s][:, h:h + 1]
        g_c = jnp.maximum(cols_all[s][:, 2 * SUBLANES + h:2 * SUBLANES + h + 1], m_prev)
        d = jnp.where(causal, jnp.exp(a_r - g_c), 0.0)
        inter_all.append(jnp.exp(m_prev - g_c))
        mt_all.append(b_c + g_c)
        w_all.append((s_all[c] * d).astype(BF16))
        g_last = g_c[L - 1:L, :]
        m_new = b_c[L - 1:L, :] + g_last
        w_s = jnp.exp(a_c - g_last)
        decay_all.append(jnp.exp(m_prev - g_last))
        kw_all.append((ks[c].astype(F32) * w_s).astype(BF16))
        m_next[s] = jnp.where(lane_m == h, m_new, m_next[s])

    wv_all = [jnp.dot(w, v, preferred_element_type=F32) for w, v in zip(w_all, vs)]
    upd_all = [lax.dot_general(kw, v, (((0,), (0,)), ((), ())), preferred_element_type=F32)
               for kw, v in zip(kw_all, vs)]

    for c, (s, h) in enumerate(chains):
        nd = wv_all[c] + inter_all[c] * qc_all[c]
        num = nd[:, 0:HEAD_DIM]
        den = nd[:, HEAD_DIM:HEAD_DIM + 1]
        hh = num / jnp.maximum(jnp.abs(den), jnp.exp(-mt_all[c]))
        mu = jnp.mean(hh, axis=-1, keepdims=True)
        dh = hh - mu
        var = jnp.mean(dh * dh, axis=-1, keepdims=True)
        o = mo_ref[s, :, HEAD_DIM * h:HEAD_DIM * (h + 1)]
        gate = 1.0 / (1.0 + jnp.exp(-o))
        hn = dh * lax.rsqrt(var + LN_EPS) * nw_ref[:, HEAD_DIM * h:HEAD_DIM * (h + 1)] * gate
        h_ref[s, :, HEAD_DIM * h:HEAD_DIM * (h + 1)] = hn.astype(BF16)
        cst[s, h] = decay_all[c] * c_old[c] + upd_all[c]

    for s in range(nb):
        mst[s] = m_next[s]
        m_ref[s] = m_next[s]
    c_ref[...] = cst[...]


def _mlstm(ml, mo, g, c0, m0, nw, layer, nseq, chunk, valid_len, nb):
    m = ml.shape[0]
    t = m // nseq
    nc = t // chunk
    up = (lax.broadcasted_iota(jnp.int32, (chunk, chunk), 0)
          <= lax.broadcasted_iota(jnp.int32, (chunk, chunk), 1)).astype(BF16)
    row = lambda width: pl.BlockSpec((nb, chunk, width), lambda b, j: (b, j, 0))
    st_c = pl.BlockSpec((nb, ML_HEADS, LANES, LANES), lambda b, j: (b, 0, 0, 0))
    st_m = pl.BlockSpec((nb, 1, LANES), lambda b, j: (b, 0, 0))
    seq3 = lambda a: a.reshape(nseq, t, a.shape[-1])
    h, c_new, m_new = pl.pallas_call(
        functools.partial(_mlstm_kernel, chunk=chunk, valid_len=valid_len, nb=nb),
        grid=(nseq // nb, nc),
        in_specs=[row(ML_COLS), row(ML_WIDTH), row(LANES), st_c, st_m, _layer_spec((1, ML_WIDTH), layer),
                  _const_spec((chunk, chunk))],
        out_specs=[row(ML_WIDTH), st_c, st_m],
        out_shape=[jax.ShapeDtypeStruct((nseq, t, ML_WIDTH), BF16),
                   jax.ShapeDtypeStruct((nseq, ML_HEADS, LANES, LANES), F32),
                   jax.ShapeDtypeStruct((nseq, 1, LANES), F32)],
        scratch_shapes=[pltpu.VMEM((nb, ML_HEADS, LANES, LANES), F32), pltpu.VMEM((nb, 1, LANES), F32)],
        compiler_params=_cparams(("arbitrary", "arbitrary"), 40),
        name="mlstm",
    )(seq3(ml), seq3(mo), seq3(g), c0, m0, nw, up)
    return h.reshape(m, ML_WIDTH), c_new, m_new


def _sb_prompt_kernel(qp_ref, kp_ref, v_ref, brow_ref, tri_ref, o_ref, acc_ref, carry_ref, q_ref, k_ref, *, tq, tk):
    i = pl.program_id(2)
    nkb = tq // tk
    tri = tri_ref[...]
    acc_ref[...] = jnp.zeros_like(acc_ref)
    carry_ref[...] = jnp.zeros_like(carry_ref)

    def head_groups(pair, extra):
        x = pair.astype(F32)
        lane = lax.broadcasted_iota(jnp.int32, x.shape, 1)
        return [jnp.where(lane < HEAD_DIM, x if hh == 0 else pltpu.roll(x, HEAD_DIM, axis=1), extra[hh]).astype(BF16)
                for hh in range(2)]

    @pl.when(i == 0)
    def _():
        lane = lax.broadcasted_iota(jnp.int32, (tq, LANES), 1)
        ones = jnp.where((lane >= HEAD_DIM) & (lane < HEAD_DIM + BIAS_LANES), 1.0, 0.0)

        def fill(r, _):
            rows = pl.ds(pl.multiple_of(r * tq, tq), tq)
            for hh, grp in enumerate(head_groups(kp_ref[rows, :], [ones, ones])):
                k_ref[rows, hh * LANES:(hh + 1) * LANES] = grp
            return 0

        lax.fori_loop(0, kp_ref.shape[0] // tq, fill, 0)

    for hh, grp in enumerate(head_groups(qp_ref[...], [brow_ref[0:1, :], brow_ref[1:2, :]])):
        q_ref[:, hh * LANES:(hh + 1) * LANES] = grp

    def steps(items):
        chains = [(n, hh) for n in range(len(items)) for hh in range(2)]
        starts = [pl.multiple_of(j * tk, tk) for j, _, _, _ in items]
        keep = []
        for j, r0, nr, masked in items:
            if masked:
                qpos = i * tq + r0 + lax.broadcasted_iota(jnp.int32, (nr, tk), 0)
                kpos = j * tk + lax.broadcasted_iota(jnp.int32, (nr, tk), 1)
                keep.append(kpos < qpos)
            else:
                keep.append(None)
        z = [lax.dot_general(q_ref[items[n][1]:items[n][1] + items[n][2], hh * LANES:(hh + 1) * LANES],
                             k_ref[pl.ds(starts[n], tk), hh * LANES:(hh + 1) * LANES],
                             (((1,), (1,)), ((), ())), preferred_element_type=F32) for n, hh in chains]
        cum = []
        for c, (n, hh) in enumerate(chains):
            sp = _softplus2(z[c])
            if keep[n] is not None:
                sp = jnp.where(keep[n], sp, 0.0)
            cum.append(jnp.dot(sp.astype(BF16), tri, preferred_element_type=F32))
        pv = []
        for c, (n, hh) in enumerate(chains):
            a = jnp.exp2(z[c] + cum[c])
            if keep[n] is not None:
                a = jnp.where(keep[n], a, 0.0)
            pv.append(jnp.dot(a.astype(BF16), v_ref[pl.ds(starts[n], tk), :], preferred_element_type=F32))
        for c, (n, hh) in enumerate(chains):
            rows = slice(items[n][1], items[n][1] + items[n][2])
            carry = carry_ref[hh, rows]
            acc_ref[hh, rows] += jnp.exp2(carry) * pv[c]
            carry_ref[hh, rows] = carry + cum[c][:, 0:1]

    assert nkb == 2
    steps([(i * nkb + 1, tk, tk, True), (i * nkb, 0, tk, True), (i * nkb, tk, tk, False)])
    full = lambda j: (j, 0, tq, False)

    top = i * nkb - 1
    one = i % 2

    @pl.when(one == 1)
    def _():
        steps([full(top - n) for n in range(2)])

    def body(t, _):
        j0 = top - 2 * one - 4 * t
        steps([full(j0 - n) for n in range(4)])
        return 0

    lax.fori_loop(0, i // 2, body, 0)

    lane = lax.broadcasted_iota(jnp.int32, (tq, LANES), 1)
    o_ref[...] = jnp.where(lane >= HEAD_DIM, acc_ref[1], acc_ref[0]).astype(BF16)


def _sb_prompt(qb, kb, vb, bias_rows, layer, nseq, tq, tk):
    m = qb.shape[0]
    t = m // nseq
    nq = t // tq
    tri = -(lax.broadcasted_iota(jnp.int32, (tk, tk), 0)
            >= lax.broadcasted_iota(jnp.int32, (tk, tk), 1)).astype(BF16)
    pair_rows = bias_rows.reshape(bias_rows.shape[0], SB_HEADS // 2, 2, LANES)
    return pl.pallas_call(
        functools.partial(_sb_prompt_kernel, tq=tq, tk=tk),
        grid=(nseq, SB_HEADS // 2, nq),
        in_specs=[pl.BlockSpec((tq, LANES), lambda b, hp, i: (b * nq + i, hp)),
                  pl.BlockSpec((t, LANES), lambda b, hp, i: (b, hp)),
                  pl.BlockSpec((t, LANES), lambda b, hp, i: (b, hp)),
                  pl.BlockSpec((None, None, 2, LANES), lambda b, hp, i: (layer, hp, 0, 0)),
                  pl.BlockSpec((tk, tk), lambda b, hp, i: (0, 0))],
        out_specs=pl.BlockSpec((tq, LANES), lambda b, hp, i: (b * nq + i, hp)),
        out_shape=jax.ShapeDtypeStruct((m, SB_WIDTH), BF16),
        scratch_shapes=[pltpu.VMEM((2, tq, LANES), F32), pltpu.VMEM((2, tq, 1), F32),
                        pltpu.VMEM((tq, 2 * LANES), BF16), pltpu.VMEM((t, 2 * LANES), BF16)],
        compiler_params=_cparams(("arbitrary", "arbitrary", "arbitrary"), 40),
        name="sb_prompt",
    )(qb, kb, vb, pair_rows, tri)


def _sb_sample_kernel(pt_ref, qbd_ref, bias_ref, kn_ref, vn_ref, tri_ref, *refs, pps, nq):
    k_refs = refs[0:pps]
    v_refs = refs[pps:2 * pps]
    o_ref = refs[2 * pps]
    acc_ref, carry_ref = refs[2 * pps + 1:]
    j = pl.program_id(1)
    qbd = qbd_ref[0]
    bias = bias_ref[...]
    tri = tri_ref[...]

    def blocks(kts, vts, keep):
        z = [jnp.dot(qbd, kt[...].astype(BF16), preferred_element_type=F32) + bias for kt in kts]
        cum = []
        for zc in z:
            sp = _softplus2(zc)
            if keep is not None:
                sp = jnp.where(keep, sp, 0.0)
            cum.append(jnp.dot(sp.astype(BF16), tri, preferred_element_type=F32))
        pv = []
        for zc, cc, vt in zip(z, cum, vts):
            a = jnp.exp2(zc + cc)
            if keep is not None:
                a = jnp.where(keep, a, 0.0)
            pv.append(lax.dot_general(a.astype(BF16), vt[...].astype(BF16), (((1,), (1,)), ((), ())),
                                      preferred_element_type=F32))
        acc = acc_ref[...]
        carry = carry_ref[...]
        for pc, cc in zip(pv, cum):
            acc = acc + jnp.exp2(carry) * pc
            carry = carry + cc[:, 0:1]
        acc_ref[...] = acc
        carry_ref[...] = carry

    @pl.when(j == 0)
    def _():
        acc_ref[...] = jnp.zeros_like(acc_ref)
        carry_ref[...] = jnp.zeros_like(carry_ref)
        t_i = lax.broadcasted_iota(jnp.int32, (SB_HEADS * nq, PAGE_SIZE), 0) % nq
        s_i = lax.broadcasted_iota(jnp.int32, (SB_HEADS * nq, PAGE_SIZE), 1)
        blocks([kn_ref.at[0]], [vn_ref.at[0]], s_i < t_i)

    blocks(k_refs, v_refs, None)

    @pl.when(j == pl.num_programs(1) - 1)
    def _():
        acc = acc_ref[...]
        lane_h = lax.broadcasted_iota(jnp.int32, (nq, SB_WIDTH), 1) // HEAD_DIM
        res = jnp.zeros((nq, SB_WIDTH), F32)
        for h in range(SB_HEADS):
            res = jnp.where(lane_h == h, acc[nq * h:nq * (h + 1), :], res)
        o_ref[0] = res


def _sb_sample(qbd, bias, kt_new, vt_new, cache_kt, cache_vt, page_table, layer, nq):
    bsz, n_pages = page_table.shape
    pps = PAGES_PER_STEP
    steps = n_pages // pps
    rows = SB_HEADS * nq
    tri = -(lax.broadcasted_iota(jnp.int32, (PAGE_SIZE, PAGE_SIZE), 0)
            >= lax.broadcasted_iota(jnp.int32, (PAGE_SIZE, PAGE_SIZE), 1)).astype(BF16)

    def page_spec(p):
        def imap(b, j, pt):
            return (layer, pt[b * n_pages + (n_pages - 1 - (j * pps + p))], 0, 0)
        return pl.BlockSpec((None, None, SB_WIDTH, PAGE_SIZE), imap)

    per_seq = lambda shape: pl.BlockSpec((1,) + shape, lambda b, j, pt: (b, 0, 0))
    grid_spec = pltpu.PrefetchScalarGridSpec(
        num_scalar_prefetch=1,
        grid=(bsz, steps),
        in_specs=[per_seq((rows, SB_WIDTH)), pl.BlockSpec((rows, PAGE_SIZE), lambda b, j, pt: (0, 0)),
                  per_seq((SB_WIDTH, PAGE_SIZE)), per_seq((SB_WIDTH, PAGE_SIZE)),
                  pl.BlockSpec((PAGE_SIZE, PAGE_SIZE), lambda b, j, pt: (0, 0))]
                 + [page_spec(p) for p in range(pps)] + [page_spec(p) for p in range(pps)],
        out_specs=per_seq((nq, SB_WIDTH)),
        scratch_shapes=[pltpu.VMEM((rows, SB_WIDTH), F32), pltpu.VMEM((rows, 1), F32)],
    )
    return pl.pallas_call(
        functools.partial(_sb_sample_kernel, pps=pps, nq=nq),
        grid_spec=grid_spec,
        out_shape=jax.ShapeDtypeStruct((bsz, nq, SB_WIDTH), F32),
        compiler_params=_cparams(("arbitrary", "arbitrary"), 40),
        name="sb_sample",
    )(page_table.reshape(-1), qbd, bias, kt_new, vt_new, tri, *([cache_kt] * pps), *([cache_vt] * pps))


def _conv3(p, s_ref, cs, w_ref, tm, halo):
    s_ref[SUBLANES:SUBLANES + tm, cs] = p
    p1 = s_ref[SUBLANES - 1:SUBLANES - 1 + tm, cs]
    p2 = s_ref[SUBLANES - 2:SUBLANES - 2 + tm, cs]
    if halo is not None:
        e1, e2, seq_len = halo
        tpos = lax.broadcasted_iota(jnp.int32, p.shape, 0) % seq_len
        p1 = jnp.where(tpos == 0, e1, p1)
        p2 = jnp.where(tpos < 2, e2, p2)
    return w_ref[0:1, cs] * p2 + w_ref[1:2, cs] * p1 + w_ref[2:3, cs] * p


def _start_tile(s_ref, tiles_per_seq):
    @pl.when(pl.program_id(0) % tiles_per_seq == 0)
    def _():
        s_ref[0:SUBLANES, :] = jnp.zeros((SUBLANES, s_ref.shape[1]), F32)


def _carry_rows(s_ref, tm):
    s_ref[0:SUBLANES, :] = s_ref[tm:tm + SUBLANES, :]


def _mix_rows(hml_ref, cv_ref, hsb_ref, x_ref, w_ref, cw_ref, g_ref, b_ref, tail_ref, s_ref, halo_refs, tm, seq_len):
    sample = halo_refs is not None
    halo = (halo_refs[0][...], halo_refs[1][...], seq_len) if sample else None
    cb = cv_ref[:, 0:CONV_DIM]
    p = cv_ref[:, CONV_DIM:2 * CONV_DIM] * cv_ref[:, 2 * CONV_DIM:3 * CONV_DIM]
    u = _conv3(p, s_ref, slice(None), cw_ref, tm, halo)
    if sample:
        tail_ref[...] = p
    else:
        tail_ref[...] = p[tm - SUBLANES:tm, :]
        _carry_rows(s_ref, tm)
    h_conv = (cb * u).astype(BF16)
    mix = (jnp.dot(hml_ref[...], w_ref[0:ML_WIDTH, :], preferred_element_type=F32)
           + jnp.dot(h_conv, w_ref[ML_WIDTH:ML_WIDTH + CONV_DIM, :], preferred_element_type=F32)
           + jnp.dot(hsb_ref[...].astype(BF16), w_ref[ML_WIDTH + CONV_DIM:, :], preferred_element_type=F32))
    return _layer_norm(DN_ALPHA * x_ref[...] + mix, g_ref[...], b_ref[...])


def _ffn_rows(x, wu_ref, cw_ref, wd_ref, g_ref, b_ref, tail_ref, s_ref, hid_ref, halo_refs, tm, seq_len):
    sample = halo_refs is not None
    if sample:
        e1_ref, e2_ref = halo_refs
    xb = x.astype(BF16)
    n_chunks = D_FF // FF_CHUNK

    def up(c):
        return (jnp.dot(xb, wu_ref[:, c * FF_CHUNK:(c + 1) * FF_CHUNK], preferred_element_type=F32),
                jnp.dot(xb, wu_ref[:, D_FF + c * FF_CHUNK:D_FF + (c + 1) * FF_CHUNK], preferred_element_type=F32))

    nxt = up(0)
    for c in range(n_chunks):
        cs = slice(c * FF_CHUNK, (c + 1) * FF_CHUNK)
        g_pre, val = nxt
        if c + 1 < n_chunks:
            nxt = up(c + 1)
        halo = (e1_ref[:, cs], e2_ref[:, cs], seq_len) if sample else None
        g_conv = _conv3(g_pre, s_ref, cs, cw_ref, tm, halo)
        hid_ref[:, cs] = (g_conv / (1.0 + jnp.exp(-g_conv)) * val).astype(BF16)
    if sample:
        tail_ref[...] = s_ref[SUBLANES:SUBLANES + tm, :]
    else:
        tail_ref[...] = s_ref[tm:tm + SUBLANES, :]
        _carry_rows(s_ref, tm)
    ff = jnp.dot(hid_ref[...], wd_ref[...], preferred_element_type=F32)
    return _layer_norm(DN_ALPHA * x + ff, g_ref[...], b_ref[...])


def _mix_ffn_kernel(*refs, tm, tiles_per_seq, seq_len):
    sample = seq_len < tm
    (hml_ref, cv_ref, hsb_ref, x_ref, wo_ref, cw_ref, g1_ref, b1_ref, wu_ref, fcw_ref, wd_ref, g2_ref,
     b2_ref) = refs[:13]
    if sample:
        e1c_ref, e2c_ref, e1f_ref, e2f_ref, y_ref, ctail_ref, ftail_ref, sc_ref, sf_ref, hid_ref = refs[13:]
        halo_c, halo_f = (e1c_ref, e2c_ref), (e1f_ref, e2f_ref)
    else:
        y_ref, ctail_ref, ftail_ref, sc_ref, sf_ref, hid_ref = refs[13:]
        halo_c = halo_f = None
    _start_tile(sc_ref, tiles_per_seq)
    _start_tile(sf_ref, tiles_per_seq)
    x1 = _mix_rows(hml_ref, cv_ref, hsb_ref, x_ref, wo_ref, cw_ref, g1_ref, b1_ref, ctail_ref, sc_ref, halo_c, tm,
                   seq_len)
    y_ref[...] = _ffn_rows(x1, wu_ref, fcw_ref, wd_ref, g2_ref, b2_ref, ftail_ref, sf_ref, hid_ref, halo_f, tm,
                           seq_len)


def _mix_ffn(hml, cv, hsb, x, wo, cw, g1, b1, wu, fcw, wd, g2, b2, layer, tm, seq_len, halo_c=None, halo_f=None):
    m = x.shape[0]
    sample = seq_len < tm
    tiles_per_seq = max(seq_len // tm, 1)
    row = lambda width: pl.BlockSpec((tm, width), lambda i: (i, 0))
    tail_rows = tm if sample else SUBLANES
    tail = lambda width: pl.BlockSpec((tail_rows, width), lambda i: (i, 0))
    in_specs = [row(ML_WIDTH), row(3 * CONV_DIM), row(SB_WIDTH), row(D_MODEL),
                _layer_spec((D_MODEL, D_MODEL), layer, pipeline_mode=pl.Buffered(1)),
                _layer_spec((SUBLANES, CONV_DIM), layer),
                _layer_spec((1, D_MODEL), layer), _layer_spec((1, D_MODEL), layer),
                _layer_spec((D_MODEL, 2 * D_FF), layer, pipeline_mode=pl.Buffered(1)),
                _layer_spec((SUBLANES, D_FF), layer),
                _layer_spec((D_FF, D_MODEL), layer, pipeline_mode=pl.Buffered(1)),
                _layer_spec((1, D_MODEL), layer), _layer_spec((1, D_MODEL), layer)]
    args = [hml, cv, hsb, x, wo, cw, g1, b1, wu, fcw, wd, g2, b2]
    if sample:
        in_specs += [row(CONV_DIM), row(CONV_DIM), row(D_FF), row(D_FF)]
        args += list(halo_c) + list(halo_f)
    return pl.pallas_call(
        functools.partial(_mix_ffn_kernel, tm=tm, tiles_per_seq=tiles_per_seq, seq_len=seq_len),
        grid=(m // tm,),
        in_specs=in_specs,
        out_specs=[row(D_MODEL), tail(CONV_DIM), tail(D_FF)],
        out_shape=[jax.ShapeDtypeStruct((m, D_MODEL), F32),
                   jax.ShapeDtypeStruct((m // tm * tail_rows, CONV_DIM), F32),
                   jax.ShapeDtypeStruct((m // tm * tail_rows, D_FF), F32)],
        scratch_shapes=[pltpu.VMEM((tm + SUBLANES, CONV_DIM), F32), pltpu.VMEM((tm + SUBLANES, D_FF), F32),
                        pltpu.VMEM((tm, D_FF), BF16)],
        compiler_params=_cparams(("arbitrary",), 56),
        name="mix_ffn",
    )(*args)


def _split3(x):
    hi = x.astype(BF16).astype(F32)
    mid = (x - hi).astype(BF16).astype(F32)
    lo = (x - hi - mid).astype(BF16).astype(F32)
    return hi, mid, lo


def _pack_w_in(w_in, b_in, sb_bias):
    depth = w_in.shape[0]
    scale = HEAD_DIM ** -0.5
    zb = lambda n: jnp.zeros((depth, n), F32)
    gi = 4 * ML_WIDTH
    cv0 = gi + 2 * ML_HEADS
    sq0 = cv0 + 3 * CONV_DIM
    sk0 = sq0 + SB_WIDTH

    def head_groups(lo, nheads, factor=None):
        seg = w_in[:, :, lo:lo + HEAD_DIM * nheads].reshape(depth, D_MODEL, nheads, HEAD_DIM)
        if factor is not None:
            seg = seg * factor
        return jnp.pad(seg, ((0, 0), (0, 0), (0, 0), (0, LANES - HEAD_DIM)))

    ml = jnp.stack([head_groups(0, ML_HEADS), head_groups(ML_WIDTH, ML_HEADS, scale),
                    head_groups(2 * ML_WIDTH, ML_HEADS)], axis=3).reshape(depth, D_MODEL, ML_COLS)
    gates = jnp.pad(w_in[:, :, gi:gi + 2 * ML_HEADS].reshape(depth, D_MODEL, 2, ML_HEADS),
                    ((0, 0), (0, 0), (0, 0), (0, SUBLANES - ML_HEADS))).reshape(depth, D_MODEL, 2 * SUBLANES)
    gates = jnp.pad(gates, ((0, 0), (0, 0), (0, LANES - 2 * SUBLANES)))
    w = jnp.concatenate([ml, w_in[:, :, 3 * ML_WIDTH:4 * ML_WIDTH], w_in[:, :, cv0:sq0],
                         w_in[:, :, sq0:sk0] * (scale * LOG2E), w_in[:, :, sk0:], gates], axis=-1).astype(BF16)

    bp = []
    for h in range(ML_HEADS):
        lo, hi = HEAD_DIM * h, HEAD_DIM * (h + 1)
        bp += [b_in[:, lo:hi], zb(HEAD_DIM)]
        bp += [b_in[:, ML_WIDTH + lo:ML_WIDTH + hi] * scale, zb(HEAD_DIM)]
        bp += [b_in[:, 2 * ML_WIDTH + lo:2 * ML_WIDTH + hi], jnp.ones((depth, 1), F32), zb(HEAD_DIM - 1)]
    bp += [b_in[:, 3 * ML_WIDTH:4 * ML_WIDTH], b_in[:, cv0:sq0], b_in[:, sq0:sk0] * (scale * LOG2E), b_in[:, sk0:],
           b_in[:, gi:gi + ML_HEADS], zb(SUBLANES - ML_HEADS),
           b_in[:, gi + ML_HEADS:gi + 2 * ML_HEADS], zb(LANES - SUBLANES - ML_HEADS)]
    b = jnp.concatenate(bp, axis=-1)[:, None, :]
    assert w.shape[-1] == PROJ_COLS and b.shape[-1] == PROJ_COLS
    wt = jnp.swapaxes(w_in[:, :, sk0:], 1, 2).astype(BF16)
    bt = b_in[:, sk0:, None]
    b2 = jnp.stack(_split3(sb_bias * LOG2E), axis=-1)
    bias_rows = jnp.pad(b2, ((0, 0), (0, 0), (HEAD_DIM, LANES - HEAD_DIM - BIAS_LANES)))
    return w, b, wt, bt, bias_rows


def _pad_rows8(a):
    return jnp.pad(a, ((0, 0), (0, SUBLANES - a.shape[1]), (0, 0)))


def _ext_state(c, n):
    ext = jnp.concatenate([c, n[..., None]], axis=-1)
    return jnp.pad(ext, ((0, 0), (0, 0), (0, LANES - HEAD_DIM), (0, LANES - HEAD_DIM - 1)))


def _halo(prev, seq_len):
    bsz, _, c = prev.shape
    z = jnp.zeros((bsz, seq_len, c), F32)
    e1 = z.at[:, 0].set(prev[:, 1])
    e2 = z.at[:, 0].set(prev[:, 0]).at[:, 1].set(prev[:, 1])
    return e1.reshape(bsz * seq_len, c), e2.reshape(bsz * seq_len, c)


def _layer_prompt(x, nseq, wts, layer):
    (w_in, b_in, wt, bt, brows, nw, cw, sbias, w_out, g1, b1, w_up, fcw, w_down, g2, b2) = wts
    m = x.shape[0]
    t = m // nseq
    ml, mo, cv, qb, kb, vb, skt, svt, g = _proj(x, w_in, b_in, wt, bt, layer, ROW_TILE, nseq, True)
    c0 = jnp.zeros((nseq, ML_HEADS, LANES, LANES), F32)
    m0 = jnp.zeros((nseq, 1, LANES), F32)
    hml, c_ext, m_new = _mlstm(ml, mo, g, c0, m0, nw, layer, nseq, MLSTM_CHUNK, MLSTM_CHUNK, MLSTM_SEQS)
    hsb = _sb_prompt(qb, kb, vb, brows, layer, nseq, ATT_TQ, ATT_TK)
    x2, ptail, gtail = _mix_ffn(hml, cv, hsb, x, w_out, cw, g1, b1, w_up, fcw, w_down, g2, b2, layer, ROW_TILE, t)
    last = lambda tail: tail.reshape(nseq, t // ROW_TILE, SUBLANES, -1)[:, -1, SUBLANES - 2:, :]
    states = (skt, svt,
              c_ext[:, :, :HEAD_DIM, :HEAD_DIM], c_ext[:, :, :HEAD_DIM, HEAD_DIM], m_new[:, 0, :ML_HEADS],
              last(ptail), last(gtail))
    return x2, states


def _layer_sample(x, nseq, wts, layer, cache_kt, cache_vt, page_table, c_prev, n_prev, m_prev, conv_prev,
                  ffn_prev):
    (w_in, b_in, wt, bt, brows, nw, cw, sbias, w_out, g1, b1, w_up, fcw, w_down, g2, b2) = wts
    m = x.shape[0]
    t = m // nseq
    ml, mo, cv, qb, _, _, sk, sv, g = _proj(x, w_in, b_in, wt, bt, layer, m, nseq, False)

    pad = lambda a: jnp.pad(a.reshape(nseq, t, -1), ((0, 0), (0, SAMPLE_PAD - t), (0, 0))).reshape(
        nseq * SAMPLE_PAD, -1)
    m0 = jnp.pad(m_prev, ((0, 0), (0, LANES - ML_HEADS)))[:, None, :]
    hml, c_ext, m_new = _mlstm(pad(ml), pad(mo), pad(g), _ext_state(c_prev, n_prev), m0, nw, layer, nseq,
                               SAMPLE_PAD, t, MLSTM_SEQS)
    hml = hml.reshape(nseq, SAMPLE_PAD, ML_WIDTH)[:, :t].reshape(m, ML_WIDTH)

    q3 = qb.reshape(nseq, t, SB_HEADS, HEAD_DIM)
    eye = jnp.eye(SB_HEADS, dtype=BF16)
    qbd = jnp.einsum("bthd,hg->bhtgd", q3, eye).reshape(nseq, SB_HEADS * t, SB_WIDTH)
    bias = jnp.broadcast_to(jnp.repeat(sbias[layer] * LOG2E, t)[:, None], (SB_HEADS * t, PAGE_SIZE))
    new_t = lambda a: jnp.pad(jnp.swapaxes(a.reshape(nseq, t, SB_WIDTH), 1, 2),
                              ((0, 0), (0, 0), (0, PAGE_SIZE - t)))
    hsb = _sb_sample(qbd, bias, new_t(sk), new_t(sv), cache_kt, cache_vt, page_table, layer, t).reshape(
        m, SB_WIDTH)

    x2, p_all, g_all = _mix_ffn(hml, cv, hsb, x, w_out, cw, g1, b1, w_up, fcw, w_down, g2, b2, layer, m, t,
                                _halo(conv_prev, t), _halo(ffn_prev, t))
    last = lambda a: a.reshape(nseq, t, -1)[:, t - 2:, :]
    states = (sk.reshape(nseq, t, SB_HEADS, HEAD_DIM), sv.reshape(nseq, t, SB_HEADS, HEAD_DIM),
              c_ext[:, :, :HEAD_DIM, :HEAD_DIM], c_ext[:, :, :HEAD_DIM, HEAD_DIM], m_new[:, 0, :ML_HEADS],
              last(p_all), last(g_all))
    return x2, states


def _feature_major_pages(cache):
    d, p = cache.shape[:2]
    return jnp.transpose(cache, (0, 1, 3, 4, 2)).reshape(d, p, SB_WIDTH, PAGE_SIZE)


def _token_major_state(kt):
    d, b, _, t = kt.shape
    return jnp.transpose(kt.reshape(d, b, SB_HEADS, HEAD_DIM, t), (0, 1, 4, 2, 3))


def kernel(x_prompt, x_sample, cache_k, cache_v, state_mlstm_c, state_mlstm_n, state_mlstm_m, state_conv,
           state_ffn_conv, page_table, w_in, b_in, mlstm_norm_w, conv_w, sb_bias, w_out, ln1_g, ln1_b, ffn_w_up,
           ffn_conv_w, ffn_w_down, ln2_g, ln2_b):
    bp, tp, _ = x_prompt.shape
    bs, ts, _ = x_sample.shape
    depth = w_in.shape[0]
    w_in_p, b_in_p, wt_p, bt_p, brows = _pack_w_in(w_in, b_in, sb_bias)
    w_out_b = w_out.astype(BF16)
    w_up_b = ffn_w_up.astype(BF16)
    w_down_b = ffn_w_down.astype(BF16)
    cw_p = _pad_rows8(conv_w)
    fcw_p = _pad_rows8(ffn_conv_w)
    ckt = _feature_major_pages(cache_k)
    cvt = _feature_major_pages(cache_v)

    yp = x_prompt.reshape(bp * tp, D_MODEL)
    ys = x_sample.reshape(bs * ts, D_MODEL)
    st_p = [[] for _ in range(7)]
    st_s = [[] for _ in range(7)]
    row3 = lambda a: a[:, None, :]
    wts = (w_in_p, b_in_p, wt_p, bt_p, brows, row3(mlstm_norm_w), cw_p, sb_bias, w_out_b, row3(ln1_g), row3(ln1_b),
           w_up_b, fcw_p, w_down_b, row3(ln2_g), row3(ln2_b))
    for l in range(depth):
        yp, new_p = _layer_prompt(yp, bp, wts, l)
        ys, new_s = _layer_sample(ys, bs, wts, l, ckt, cvt, page_table, state_mlstm_c[l], state_mlstm_n[l],
                                  state_mlstm_m[l], state_conv[l], state_ffn_conv[l])
        for lst, a in zip(st_p, new_p):
            lst.append(a)
        for lst, a in zip(st_s, new_s):
            lst.append(a)
    outs_p = [jnp.stack(s) for s in st_p]
    outs_p[0] = _token_major_state(outs_p[0])
    outs_p[1] = _token_major_state(outs_p[1])
    outs_s = [jnp.stack(s) for s in st_s]
    return (yp.reshape(bp, tp, D_MODEL), ys.reshape(bs, ts, D_MODEL), *outs_p, *outs_s)
```

```python
import functools

import jax
import jax.numpy as jnp
from jax import lax
from jax.experimental import pallas as pl
from jax.experimental.pallas import tpu as pltpu

F32 = jnp.float32
BF16 = jnp.bfloat16

D_MODEL = 1024
DEPTH = 4
ML_HEADS = 4
HEAD_DIM = 64
ML_WIDTH = ML_HEADS * HEAD_DIM
CONV_DIM = 256
CONV_WIDTH = 3
SB_HEADS = 8
SB_WIDTH = SB_HEADS * HEAD_DIM
D_FF = 2816
PAGE_SIZE = 128
LN_EPS = 1e-5
DN_ALPHA = (2 * DEPTH) ** 0.25

LANES = 128
SUBLANES = 8
MIB = 1024 * 1024

C_ML = 0
ML_COLS = ML_HEADS * 3 * LANES
C_MO = C_ML + ML_COLS
C_CV = C_MO + ML_WIDTH
C_SQ = C_CV + 3 * CONV_DIM
C_SK = C_SQ + SB_WIDTH
C_SV = C_SK + SB_WIDTH
C_G = C_SV + SB_WIDTH
PROJ_COLS = C_G + LANES
LOG2E = 1.4426950408889634
BIAS_LANES = 3

MLSTM_CHUNK = 512
MLSTM_SEQS = 1
SAMPLE_PAD = 128
ROW_TILE = 512
ATT_TQ = 512
ATT_TK = 256
PAGES_PER_STEP = 16
FF_CHUNK = 256


def _cparams(sem, vmem_mib):
    return pltpu.CompilerParams(dimension_semantics=sem, vmem_limit_bytes=vmem_mib * MIB)


def _const_spec(shape):
    zeros = (0,) * len(shape)
    return pl.BlockSpec(shape, lambda *_: zeros)


def _layer_spec(shape, layer, **kwargs):
    index = (layer,) + (0,) * len(shape)
    return pl.BlockSpec((None,) + tuple(shape), lambda *_: index, **kwargs)


def _softplus(z):
    return jnp.maximum(z, 0.0) + jnp.log(1.0 + jnp.exp(-jnp.abs(z)))


def _softplus2(z):
    sign_bit = jnp.uint32(0x80000000)
    neg_abs = lax.bitcast_convert_type(lax.bitcast_convert_type(z, jnp.uint32) | sign_bit, F32)
    return jnp.maximum(z, 0.0) + jnp.log2(1.0 + jnp.exp2(neg_abs))


def _layer_norm(y, g, b):
    mu = jnp.mean(y, axis=-1, keepdims=True)
    d = y - mu
    var = jnp.mean(d * d, axis=-1, keepdims=True)
    return d * lax.rsqrt(var + LN_EPS) * g + b


def _proj_kernel(x_ref, w_ref, b_ref, wt_ref, bt_ref, ml_ref, mo_ref, cv_ref, qb_ref, kb_ref, vb_ref, k_ref, v_ref,
                 g_ref, *, kv_transposed):
    x = x_ref[...].astype(BF16)

    def mm(c0, c1):
        return jnp.dot(x, w_ref[:, c0:c1], preferred_element_type=F32) + b_ref[:, c0:c1]

    for c in range(0, ML_COLS, 512):
        ml_ref[:, c:c + 512] = mm(C_ML + c, C_ML + c + 512).astype(BF16)
    mo_ref[...] = mm(C_MO, C_MO + ML_WIDTH)
    for c in range(0, 3 * CONV_DIM, CONV_DIM):
        cv_ref[:, c:c + CONV_DIM] = mm(C_CV + c, C_CV + c + CONV_DIM)
    qb_ref[...] = mm(C_SQ, C_SQ + SB_WIDTH).astype(BF16)
    k = mm(C_SK, C_SK + SB_WIDTH)
    kb_ref[...] = k.astype(BF16)
    v = mm(C_SV, C_SV + SB_WIDTH)
    vb_ref[...] = v.astype(BF16)
    if kv_transposed:
        for r, ref in ((0, k_ref), (SB_WIDTH, v_ref)):
            ref[...] = lax.dot_general(wt_ref[r:r + SB_WIDTH, :], x, (((1,), (1,)), ((), ())),
                                       preferred_element_type=F32) + bt_ref[r:r + SB_WIDTH, :]
    else:
        k_ref[...] = k
        v_ref[...] = v
    g_ref[...] = mm(C_G, C_G + LANES)


def _proj(x, w, b, wt, bt, layer, tm, nseq, kv_transposed):
    m = x.shape[0]
    t = m // nseq
    row = lambda width: pl.BlockSpec((tm, width), lambda i: (i, 0))
    outs = [(ML_COLS, BF16), (ML_WIDTH, F32), (3 * CONV_DIM, F32), (SB_WIDTH, BF16), (SB_WIDTH, BF16),
            (SB_WIDTH, BF16), (SB_WIDTH, F32), (SB_WIDTH, F32), (LANES, F32)]
    out_specs = [row(wd) for wd, _ in outs]
    out_shape = [jax.ShapeDtypeStruct((m, wd), dt) for wd, dt in outs]
    if kv_transposed:
        tiles = t // tm
        for idx in (6, 7):
            out_specs[idx] = pl.BlockSpec((None, SB_WIDTH, tm), lambda i: (i // tiles, 0, i % tiles))
            out_shape[idx] = jax.ShapeDtypeStruct((nseq, SB_WIDTH, t), F32)
    return pl.pallas_call(
        functools.partial(_proj_kernel, kv_transposed=kv_transposed),
        grid=(m // tm,),
        in_specs=[row(D_MODEL),
                  _layer_spec((D_MODEL, PROJ_COLS), layer, pipeline_mode=pl.Buffered(1)),
                  _layer_spec((1, PROJ_COLS), layer),
                  _layer_spec((2 * SB_WIDTH, D_MODEL), layer), _layer_spec((2 * SB_WIDTH, 1), layer)],
        out_specs=out_specs,
        out_shape=out_shape,
        compiler_params=_cparams(("arbitrary",), 56),
        name="proj",
    )(x, w, b, wt, bt)


def _mlstm_kernel(ml_ref, mo_ref, g_ref, c0_ref, m0_ref, nw_ref, up_ref, h_ref, c_ref, m_ref, cst, mst,
                  *, chunk, valid_len, nb):
    L = chunk

    @pl.when(pl.program_id(1) == 0)
    def _():
        cst[...] = c0_ref[...]
        mst[...] = m0_ref[...]

    up = up_ref[...]
    row_i = lax.broadcasted_iota(jnp.int32, (L, L), 0)
    col_i = lax.broadcasted_iota(jnp.int32, (L, L), 1)
    causal = row_i >= col_i
    lane_m = lax.broadcasted_iota(jnp.int32, (1, LANES), 1)

    a_rows, cols_all, m_prev_all = [], [], []
    for s in range(nb):
        gt = g_ref[s].T
        t_ig = gt[0:SUBLANES, :]
        t_f = gt[SUBLANES:2 * SUBLANES, :]
        lf = -_softplus(-t_f)
        if valid_len < L:
            pos = lax.broadcasted_iota(jnp.int32, (SUBLANES, L), 1)
            t_ig = jnp.where(pos < valid_len, t_ig, -1e30)
            lf = jnp.where(pos < valid_len, lf, 0.0)
        hi = lf.astype(BF16)
        r1 = lf - hi.astype(F32)
        mid = r1.astype(BF16)
        lo = (r1 - mid.astype(F32)).astype(BF16)
        b_row = (jnp.dot(hi, up, preferred_element_type=F32) + jnp.dot(mid, up, preferred_element_type=F32)
                 + jnp.dot(lo, up, preferred_element_type=F32))
        a_row = t_ig - b_row
        a_rows.append(a_row)
        pos_l = lax.broadcasted_iota(jnp.int32, (SUBLANES, L), 1)
        cmax = a_row
        shift = 1
        while shift < L:
            cmax = jnp.maximum(cmax, jnp.where(pos_l >= shift, pltpu.roll(cmax, shift, axis=1), -jnp.inf))
            shift *= 2
        cols_all.append(jnp.concatenate([a_row, b_row, cmax, jnp.zeros((LANES - 3 * SUBLANES, L), F32)],
                                        axis=0).T)
        m_prev_all.append(mst[s])

    chains = [(s, h) for s in range(nb) for h in range(ML_HEADS)]
    grp = lambda h, part: slice(3 * LANES * h + part * LANES, 3 * LANES * h + (part + 1) * LANES)
    qs = [ml_ref[s, :, grp(h, 0)] for s, h in chains]
    ks = [ml_ref[s, :, grp(h, 1)] for s, h in chains]
    vs = [ml_ref[s, :, grp(h, 2)] for s, h in chains]
    c_old = [cst[s, h] for s, h in chains]
    s_all = [lax.dot_general(q, k, (((1,), (1,)), ((), ())), preferred_element_type=F32) for q, k in zip(qs, ks)]
    qc_all = [jnp.dot(q, c.astype(BF16), preferred_element_type=F32) for q, c in zip(qs, c_old)]

    w_all, kw_all, inter_all, mt_all, decay_all = [], [], [], [], []
    m_next = list(m_prev_all)
    for c, (s, h) in enumerate(chains):
        a_c = cols_all[s][:, h:h + 1]
        b_c = cols_all[s][:, SUBLANES + h:SUBLANES + h + 1]
        a_r = a_rows[s][h:h + 1, :]
        m_prev = m_prev_all[s][:, h:h + 1]
        g_c = jnp.maximum(cols_all[s][:, 2 * SUBLANES + h:2 * SUBLANES + h + 1], m_prev)
        d = jnp.where(causal, jnp.exp(a_r - g_c), 0.0)
        inter_all.append(jnp.exp(m_prev - g_c))
        mt_all.append(b_c + g_c)
        w_all.append((s_all[c] * d).astype(BF16))
        g_last = g_c[L - 1:L, :]
        m_new = b_c[L - 1:L, :] + g_last
        w_s = jnp.exp(a_c - g_last)
        decay_all.append(jnp.exp(m_prev - g_last))
        kw_all.append((ks[c].astype(F32) * w_s).astype(BF16))
        m_next[s] = jnp.where(lane_m == h, m_new, m_next[s])

    wv_all = [jnp.dot(w, v, preferred_element_type=F32) for w, v in zip(w_all, vs)]
    upd_all = [lax.dot_general(kw, v, (((0,), (0,)), ((), ())), preferred_element_type=F32)
               for kw, v in zip(kw_all, vs)]

    jmat = jnp.where(lax.broadcasted_iota(jnp.int32, (LANES, LANES), 0) < HEAD_DIM, 1.0 / HEAD_DIM, 0.0).astype(BF16)

    def lane_mean(x):
        hi = x.astype(BF16)
        lo = (x - hi.astype(F32)).astype(BF16)
        return jnp.dot(hi, jmat, preferred_element_type=F32) + jnp.dot(lo, jmat, preferred_element_type=F32)

    hh_all = []
    for c, (s, h) in enumerate(chains):
        nd = wv_all[c] + inter_all[c] * qc_all[c]
        den = nd[:, HEAD_DIM:HEAD_DIM + 1]
        hh_all.append(nd / jnp.maximum(jnp.abs(den), jnp.exp(-mt_all[c])))
    mu_all = [lane_mean(hh) for hh in hh_all]
    dh_all = [hh - mu for hh, mu in zip(hh_all, mu_all)]
    var_all = [lane_mean(dh * dh) for dh in dh_all]
    for c, (s, h) in enumerate(chains):
        o = mo_ref[s, :, HEAD_DIM * h:HEAD_DIM * (h + 1)]
        gate = 1.0 / (1.0 + jnp.exp(-o))
        hn = (dh_all[c] * lax.rsqrt(var_all[c] + LN_EPS))[:, 0:HEAD_DIM] * nw_ref[:, HEAD_DIM * h:HEAD_DIM * (h + 1)]
        h_ref[s, :, HEAD_DIM * h:HEAD_DIM * (h + 1)] = (hn * gate).astype(BF16)
        cst[s, h] = decay_all[c] * c_old[c] + upd_all[c]

    for s in range(nb):
        mst[s] = m_next[s]
        m_ref[s] = m_next[s]
    c_ref[...] = cst[...]


def _mlstm(ml, mo, g, c0, m0, nw, layer, nseq, chunk, valid_len, nb):
    m = ml.shape[0]
    t = m // nseq
    nc = t // chunk
    up = (lax.broadcasted_iota(jnp.int32, (chunk, chunk), 0)
          <= lax.broadcasted_iota(jnp.int32, (chunk, chunk), 1)).astype(BF16)
    row = lambda width: pl.BlockSpec((nb, chunk, width), lambda b, j: (b, j, 0))
    st_c = pl.BlockSpec((nb, ML_HEADS, LANES, LANES), lambda b, j: (b, 0, 0, 0))
    st_m = pl.BlockSpec((nb, 1, LANES), lambda b, j: (b, 0, 0))
    seq3 = lambda a: a.reshape(nseq, t, a.shape[-1])
    h, c_new, m_new = pl.pallas_call(
        functools.partial(_mlstm_kernel, chunk=chunk, valid_len=valid_len, nb=nb),
        grid=(nseq // nb, nc),
        in_specs=[row(ML_COLS), row(ML_WIDTH), row(LANES), st_c, st_m, _layer_spec((1, ML_WIDTH), layer),
                  _const_spec((chunk, chunk))],
        out_specs=[row(ML_WIDTH), st_c, st_m],
        out_shape=[jax.ShapeDtypeStruct((nseq, t, ML_WIDTH), BF16),
                   jax.ShapeDtypeStruct((nseq, ML_HEADS, LANES, LANES), F32),
                   jax.ShapeDtypeStruct((nseq, 1, LANES), F32)],
        scratch_shapes=[pltpu.VMEM((nb, ML_HEADS, LANES, LANES), F32), pltpu.VMEM((nb, 1, LANES), F32)],
        compiler_params=_cparams(("arbitrary", "arbitrary"), 40),
        name="mlstm",
    )(seq3(ml), seq3(mo), seq3(g), c0, m0, nw, up)
    return h.reshape(m, ML_WIDTH), c_new, m_new


def _sb_prompt_kernel(qp_ref, kp_ref, v_ref, brow_ref, tri_ref, o_ref, acc_ref, carry_ref, q_ref, k_ref, *, tq, tk):
    i = pl.program_id(2)
    nkb = tq // tk
    tri = tri_ref[...]
    acc_ref[...] = jnp.zeros_like(acc_ref)
    carry_ref[...] = jnp.zeros_like(carry_ref)

    def head_groups(pair, extra):
        x = pair.astype(F32)
        lane = lax.broadcasted_iota(jnp.int32, x.shape, 1)
        return [jnp.where(lane < HEAD_DIM, x if hh == 0 else pltpu.roll(x, HEAD_DIM, axis=1), extra[hh]).astype(BF16)
                for hh in range(2)]

    @pl.when(i == 0)
    def _():
        lane = lax.broadcasted_iota(jnp.int32, (tq, LANES), 1)
        ones = jnp.where((lane >= HEAD_DIM) & (lane < HEAD_DIM + BIAS_LANES), 1.0, 0.0)

        def fill(r, _):
            rows = pl.ds(pl.multiple_of(r * tq, tq), tq)
            for hh, grp in enumerate(head_groups(kp_ref[rows, :], [ones, ones])):
                k_ref[rows, hh * LANES:(hh + 1) * LANES] = grp
            return 0

        lax.fori_loop(0, kp_ref.shape[0] // tq, fill, 0)

    for hh, grp in enumerate(head_groups(qp_ref[...], [brow_ref[0:1, :], brow_ref[1:2, :]])):
        q_ref[:, hh * LANES:(hh + 1) * LANES] = grp

    def steps(items):
        chains = [(n, hh) for n in range(len(items)) for hh in range(2)]
        starts = [pl.multiple_of(j * tk, tk) for j, _, _, _ in items]
        keep = []
        for j, r0, nr, masked in items:
            if masked:
                qpos = i * tq + r0 + lax.broadcasted_iota(jnp.int32, (nr, tk), 0)
                kpos = j * tk + lax.broadcasted_iota(jnp.int32, (nr, tk), 1)
                keep.append(kpos < qpos)
            else:
                keep.append(None)
        z = [lax.dot_general(q_ref[items[n][1]:items[n][1] + items[n][2], hh * LANES:(hh + 1) * LANES],
                             k_ref[pl.ds(starts[n], tk), hh * LANES:(hh + 1) * LANES],
                             (((1,), (1,)), ((), ())), preferred_element_type=F32) for n, hh in chains]
        cum = []
        for c, (n, hh) in enumerate(chains):
            sp = _softplus2(z[c])
            if keep[n] is not None:
                sp = jnp.where(keep[n], sp, 0.0)
            cum.append(jnp.dot(sp.astype(BF16), tri, preferred_element_type=F32))
        pv = []
        for c, (n, hh) in enumerate(chains):
            a = jnp.exp2(z[c] + cum[c])
            if keep[n] is not None:
                a = jnp.where(keep[n], a, 0.0)
            pv.append(jnp.dot(a.astype(BF16), v_ref[pl.ds(starts[n], tk), :], preferred_element_type=F32))
        for c, (n, hh) in enumerate(chains):
            rows = slice(items[n][1], items[n][1] + items[n][2])
            carry = carry_ref[hh, rows]
            acc_ref[hh, rows] += jnp.exp2(carry) * pv[c]
            carry_ref[hh, rows] = carry + cum[c][:, 0:1]

    assert nkb == 2
    steps([(i * nkb + 1, tk, tk, True), (i * nkb, 0, tk, True), (i * nkb, tk, tk, False)])
    full = lambda j: (j, 0, tq, False)

    top = i * nkb - 1
    one = i % 2

    @pl.when(one == 1)
    def _():
        steps([full(top - n) for n in range(2)])

    def body(t, _):
        j0 = top - 2 * one - 4 * t
        steps([full(j0 - n) for n in range(4)])
        return 0

    lax.fori_loop(0, i // 2, body, 0)

    lane = lax.broadcasted_iota(jnp.int32, (tq, LANES), 1)
    o_ref[...] = jnp.where(lane >= HEAD_DIM, acc_ref[1], acc_ref[0]).astype(BF16)


def _sb_prompt(qb, kb, vb, bias_rows, layer, nseq, tq, tk):
    m = qb.shape[0]
    t = m // nseq
    nq = t // tq
    tri = -(lax.broadcasted_iota(jnp.int32, (tk, tk), 0)
            >= lax.broadcasted_iota(jnp.int32, (tk, tk), 1)).astype(BF16)
    pair_rows = bias_rows.reshape(bias_rows.shape[0], SB_HEADS // 2, 2, LANES)
    return pl.pallas_call(
        functools.partial(_sb_prompt_kernel, tq=tq, tk=tk),
        grid=(nseq, SB_HEADS // 2, nq),
        in_specs=[pl.BlockSpec((tq, LANES), lambda b, hp, i: (b * nq + i, hp)),
                  pl.BlockSpec((t, LANES), lambda b, hp, i: (b, hp)),
                  pl.BlockSpec((t, LANES), lambda b, hp, i: (b, hp)),
                  pl.BlockSpec((None, None, 2, LANES), lambda b, hp, i: (layer, hp, 0, 0)),
                  pl.BlockSpec((tk, tk), lambda b, hp, i: (0, 0))],
        out_specs=pl.BlockSpec((tq, LANES), lambda b, hp, i: (b * nq + i, hp)),
        out_shape=jax.ShapeDtypeStruct((m, SB_WIDTH), BF16),
        scratch_shapes=[pltpu.VMEM((2, tq, LANES), F32), pltpu.VMEM((2, tq, 1), F32),
                        pltpu.VMEM((tq, 2 * LANES), BF16), pltpu.VMEM((t, 2 * LANES), BF16)],
        compiler_params=_cparams(("arbitrary", "arbitrary", "arbitrary"), 40),
        name="sb_prompt",
    )(qb, kb, vb, pair_rows, tri)


def _sb_sample_kernel(pt_ref, qbd_ref, bias_ref, kn_ref, vn_ref, tri_ref, *refs, pps, nq):
    k_refs = refs[0:pps]
    v_refs = refs[pps:2 * pps]
    o_ref = refs[2 * pps]
    acc_ref, carry_ref = refs[2 * pps + 1:]
    j = pl.program_id(1)
    qbd = qbd_ref[0]
    bias = bias_ref[...]
    tri = tri_ref[...]

    def blocks(kts, vts, keep):
        z = [jnp.dot(qbd, kt[...].astype(BF16), preferred_element_type=F32) + bias for kt in kts]
        cum = []
        for zc in z:
            sp = _softplus2(zc)
            if keep is not None:
                sp = jnp.where(keep, sp, 0.0)
            cum.append(jnp.dot(sp.astype(BF16), tri, preferred_element_type=F32))
        pv = []
        for zc, cc, vt in zip(z, cum, vts):
            a = jnp.exp2(zc + cc)
            if keep is not None:
                a = jnp.where(keep, a, 0.0)
            pv.append(lax.dot_general(a.astype(BF16), vt[...].astype(BF16), (((1,), (1,)), ((), ())),
                                      preferred_element_type=F32))
        acc = acc_ref[...]
        carry = carry_ref[...]
        for pc, cc in zip(pv, cum):
            acc = acc + jnp.exp2(carry) * pc
            carry = carry + cc[:, 0:1]
        acc_ref[...] = acc
        carry_ref[...] = carry

    @pl.when(j == 0)
    def _():
        acc_ref[...] = jnp.zeros_like(acc_ref)
        carry_ref[...] = jnp.zeros_like(carry_ref)
        t_i = lax.broadcasted_iota(jnp.int32, (SB_HEADS * nq, PAGE_SIZE), 0) % nq
        s_i = lax.broadcasted_iota(jnp.int32, (SB_HEADS * nq, PAGE_SIZE), 1)
        blocks([kn_ref.at[0]], [vn_ref.at[0]], s_i < t_i)

    blocks(k_refs, v_refs, None)

    @pl.when(j == pl.num_programs(1) - 1)
    def _():
        acc = acc_ref[...]
        lane_h = lax.broadcasted_iota(jnp.int32, (nq, SB_WIDTH), 1) // HEAD_DIM
        res = jnp.zeros((nq, SB_WIDTH), F32)
        for h in range(SB_HEADS):
            res = jnp.where(lane_h == h, acc[nq * h:nq * (h + 1), :], res)
        o_ref[0] = res


def _sb_sample(qbd, bias, kt_new, vt_new, cache_kt, cache_vt, page_table, layer, nq):
    bsz, n_pages = page_table.shape
    pps = PAGES_PER_STEP
    steps = n_pages // pps
    rows = SB_HEADS * nq
    tri = -(lax.broadcasted_iota(jnp.int32, (PAGE_SIZE, PAGE_SIZE), 0)
            >= lax.broadcasted_iota(jnp.int32, (PAGE_SIZE, PAGE_SIZE), 1)).astype(BF16)

    def page_spec(p):
        def imap(b, j, pt):
            return (layer, pt[b * n_pages + (n_pages - 1 - (j * pps + p))], 0, 0)
        return pl.BlockSpec((None, None, SB_WIDTH, PAGE_SIZE), imap)

    per_seq = lambda shape: pl.BlockSpec((1,) + shape, lambda b, j, pt: (b, 0, 0))
    grid_spec = pltpu.PrefetchScalarGridSpec(
        num_scalar_prefetch=1,
        grid=(bsz, steps),
        in_specs=[per_seq((rows, SB_WIDTH)), pl.BlockSpec((rows, PAGE_SIZE), lambda b, j, pt: (0, 0)),
                  per_seq((SB_WIDTH, PAGE_SIZE)), per_seq((SB_WIDTH, PAGE_SIZE)),
                  pl.BlockSpec((PAGE_SIZE, PAGE_SIZE), lambda b, j, pt: (0, 0))]
                 + [page_spec(p) for p in range(pps)] + [page_spec(p) for p in range(pps)],
        out_specs=per_seq((nq, SB_WIDTH)),
        scratch_shapes=[pltpu.VMEM((rows, SB_WIDTH), F32), pltpu.VMEM((rows, 1), F32)],
    )
    return pl.pallas_call(
        functools.partial(_sb_sample_kernel, pps=pps, nq=nq),
        grid_spec=grid_spec,
        out_shape=jax.ShapeDtypeStruct((bsz, nq, SB_WIDTH), F32),
        compiler_params=_cparams(("arbitrary", "arbitrary"), 40),
        name="sb_sample",
    )(page_table.reshape(-1), qbd, bias, kt_new, vt_new, tri, *([cache_kt] * pps), *([cache_vt] * pps))


def _conv3(p, s_ref, cs, w_ref, tm, halo):
    s_ref[SUBLANES:SUBLANES + tm, cs] = p
    p1 = s_ref[SUBLANES - 1:SUBLANES - 1 + tm, cs]
    p2 = s_ref[SUBLANES - 2:SUBLANES - 2 + tm, cs]
    if halo is not None:
        e1, e2, seq_len = halo
        tpos = lax.broadcasted_iota(jnp.int32, p.shape, 0) % seq_len
        p1 = jnp.where(tpos == 0, e1, p1)
        p2 = jnp.where(tpos < 2, e2, p2)
    return w_ref[0:1, cs] * p2 + w_ref[1:2, cs] * p1 + w_ref[2:3, cs] * p


def _start_tile(s_ref, tiles_per_seq):
    @pl.when(pl.program_id(0) % tiles_per_seq == 0)
    def _():
        s_ref[0:SUBLANES, :] = jnp.zeros((SUBLANES, s_ref.shape[1]), F32)


def _carry_rows(s_ref, tm):
    s_ref[0:SUBLANES, :] = s_ref[tm:tm + SUBLANES, :]


def _mix_rows(hml_ref, cv_ref, hsb_ref, x_ref, w_ref, cw_ref, g_ref, b_ref, tail_ref, s_ref, halo_refs, tm, seq_len):
    sample = halo_refs is not None
    halo = (halo_refs[0][...], halo_refs[1][...], seq_len) if sample else None
    cb = cv_ref[:, 0:CONV_DIM]
    p = cv_ref[:, CONV_DIM:2 * CONV_DIM] * cv_ref[:, 2 * CONV_DIM:3 * CONV_DIM]
    u = _conv3(p, s_ref, slice(None), cw_ref, tm, halo)
    if sample:
        tail_ref[...] = p
    else:
        tail_ref[...] = p[tm - SUBLANES:tm, :]
        _carry_rows(s_ref, tm)
    h_conv = (cb * u).astype(BF16)
    mix = (jnp.dot(hml_ref[...], w_ref[0:ML_WIDTH, :], preferred_element_type=F32)
           + jnp.dot(h_conv, w_ref[ML_WIDTH:ML_WIDTH + CONV_DIM, :], preferred_element_type=F32)
           + jnp.dot(hsb_ref[...].astype(BF16), w_ref[ML_WIDTH + CONV_DIM:, :], preferred_element_type=F32))
    return _layer_norm(DN_ALPHA * x_ref[...] + mix, g_ref[...], b_ref[...])


def _ffn_rows(x, wu_ref, cw_ref, wd_ref, g_ref, b_ref, tail_ref, s_ref, hid_ref, halo_refs, tm, seq_len):
    sample = halo_refs is not None
    if sample:
        e1_ref, e2_ref = halo_refs
    xb = x.astype(BF16)
    n_chunks = D_FF // FF_CHUNK

    def up(c):
        return (jnp.dot(xb, wu_ref[:, c * FF_CHUNK:(c + 1) * FF_CHUNK], preferred_element_type=F32),
                jnp.dot(xb, wu_ref[:, D_FF + c * FF_CHUNK:D_FF + (c + 1) * FF_CHUNK], preferred_element_type=F32))

    nxt = up(0)
    for c in range(n_chunks):
        cs = slice(c * FF_CHUNK, (c + 1) * FF_CHUNK)
        g_pre, val = nxt
        if c + 1 < n_chunks:
            nxt = up(c + 1)
        halo = (e1_ref[:, cs], e2_ref[:, cs], seq_len) if sample else None
        g_conv = _conv3(g_pre, s_ref, cs, cw_ref, tm, halo)
        hid_ref[:, cs] = (g_conv / (1.0 + jnp.exp(-g_conv)) * val).astype(BF16)
    if sample:
        tail_ref[...] = s_ref[SUBLANES:SUBLANES + tm, :]
    else:
        tail_ref[...] = s_ref[tm:tm + SUBLANES, :]
        _carry_rows(s_ref, tm)
    ff = jnp.dot(hid_ref[...], wd_ref[...], preferred_element_type=F32)
    return _layer_norm(DN_ALPHA * x + ff, g_ref[...], b_ref[...])


def _mix_ffn_kernel(*refs, tm, tiles_per_seq, seq_len):
    sample = seq_len < tm
    (hml_ref, cv_ref, hsb_ref, x_ref, wo_ref, cw_ref, g1_ref, b1_ref, wu_ref, fcw_ref, wd_ref, g2_ref,
     b2_ref) = refs[:13]
    if sample:
        e1c_ref, e2c_ref, e1f_ref, e2f_ref, y_ref, ctail_ref, ftail_ref, sc_ref, sf_ref, hid_ref = refs[13:]
        halo_c, halo_f = (e1c_ref, e2c_ref), (e1f_ref, e2f_ref)
    else:
        y_ref, ctail_ref, ftail_ref, sc_ref, sf_ref, hid_ref = refs[13:]
        halo_c = halo_f = None
    _start_tile(sc_ref, tiles_per_seq)
    _start_tile(sf_ref, tiles_per_seq)
    x1 = _mix_rows(hml_ref, cv_ref, hsb_ref, x_ref, wo_ref, cw_ref, g1_ref, b1_ref, ctail_ref, sc_ref, halo_c, tm,
                   seq_len)
    y_ref[...] = _ffn_rows(x1, wu_ref, fcw_ref, wd_ref, g2_ref, b2_ref, ftail_ref, sf_ref, hid_ref, halo_f, tm,
                           seq_len)


def _mix_ffn(hml, cv, hsb, x, wo, cw, g1, b1, wu, fcw, wd, g2, b2, layer, tm, seq_len, halo_c=None, halo_f=None):
    m = x.shape[0]
    sample = seq_len < tm
    tiles_per_seq = max(seq_len // tm, 1)
    row = lambda width: pl.BlockSpec((tm, width), lambda i: (i, 0))
    tail_rows = tm if sample else SUBLANES
    tail = lambda width: pl.BlockSpec((tail_rows, width), lambda i: (i, 0))
    in_specs = [row(ML_WIDTH), row(3 * CONV_DIM), row(SB_WIDTH), row(D_MODEL),
                _layer_spec((D_MODEL, D_MODEL), layer, pipeline_mode=pl.Buffered(1)),
                _layer_spec((SUBLANES, CONV_DIM), layer),
                _layer_spec((1, D_MODEL), layer), _layer_spec((1, D_MODEL), layer),
                _layer_spec((D_MODEL, 2 * D_FF), layer, pipeline_mode=pl.Buffered(1)),
                _layer_spec((SUBLANES, D_FF), layer),
                _layer_spec((D_FF, D_MODEL), layer, pipeline_mode=pl.Buffered(1)),
                _layer_spec((1, D_MODEL), layer), _layer_spec((1, D_MODEL), layer)]
    args = [hml, cv, hsb, x, wo, cw, g1, b1, wu, fcw, wd, g2, b2]
    if sample:
        in_specs += [row(CONV_DIM), row(CONV_DIM), row(D_FF), row(D_FF)]
        args += list(halo_c) + list(halo_f)
    return pl.pallas_call(
        functools.partial(_mix_ffn_kernel, tm=tm, tiles_per_seq=tiles_per_seq, seq_len=seq_len),
        grid=(m // tm,),
        in_specs=in_specs,
        out_specs=[row(D_MODEL), tail(CONV_DIM), tail(D_FF)],
        out_shape=[jax.ShapeDtypeStruct((m, D_MODEL), F32),
                   jax.ShapeDtypeStruct((m // tm * tail_rows, CONV_DIM), F32),
                   jax.ShapeDtypeStruct((m // tm * tail_rows, D_FF), F32)],
        scratch_shapes=[pltpu.VMEM((tm + SUBLANES, CONV_DIM), F32), pltpu.VMEM((tm + SUBLANES, D_FF), F32),
                        pltpu.VMEM((tm, D_FF), BF16)],
        compiler_params=_cparams(("arbitrary",), 56),
        name="mix_ffn",
    )(*args)


def _split3(x):
    hi = x.astype(BF16).astype(F32)
    mid = (x - hi).astype(BF16).astype(F32)
    lo = (x - hi - mid).astype(BF16).astype(F32)
    return hi, mid, lo


def _pack_w_in(w_in, b_in, sb_bias):
    depth = w_in.shape[0]
    scale = HEAD_DIM ** -0.5
    zb = lambda n: jnp.zeros((depth, n), F32)
    gi = 4 * ML_WIDTH
    cv0 = gi + 2 * ML_HEADS
    sq0 = cv0 + 3 * CONV_DIM
    sk0 = sq0 + SB_WIDTH

    def head_groups(lo, nheads, factor=None):
        seg = w_in[:, :, lo:lo + HEAD_DIM * nheads].reshape(depth, D_MODEL, nheads, HEAD_DIM)
        if factor is not None:
            seg = seg * factor
        return jnp.pad(seg, ((0, 0), (0, 0), (0, 0), (0, LANES - HEAD_DIM)))

    ml = jnp.stack([head_groups(0, ML_HEADS), head_groups(ML_WIDTH, ML_HEADS, scale),
                    head_groups(2 * ML_WIDTH, ML_HEADS)], axis=3).reshape(depth, D_MODEL, ML_COLS)
    gates = jnp.pad(w_in[:, :, gi:gi + 2 * ML_HEADS].reshape(depth, D_MODEL, 2, ML_HEADS),
                    ((0, 0), (0, 0), (0, 0), (0, SUBLANES - ML_HEADS))).reshape(depth, D_MODEL, 2 * SUBLANES)
    gates = jnp.pad(gates, ((0, 0), (0, 0), (0, LANES - 2 * SUBLANES)))
    w = jnp.concatenate([ml, w_in[:, :, 3 * ML_WIDTH:4 * ML_WIDTH], w_in[:, :, cv0:sq0],
                         w_in[:, :, sq0:sk0] * (scale * LOG2E), w_in[:, :, sk0:], gates], axis=-1).astype(BF16)

    bp = []
    for h in range(ML_HEADS):
        lo, hi = HEAD_DIM * h, HEAD_DIM * (h + 1)
        bp += [b_in[:, lo:hi], zb(HEAD_DIM)]
        bp += [b_in[:, ML_WIDTH + lo:ML_WIDTH + hi] * scale, zb(HEAD_DIM)]
        bp += [b_in[:, 2 * ML_WIDTH + lo:2 * ML_WIDTH + hi], jnp.ones((depth, 1), F32), zb(HEAD_DIM - 1)]
    bp += [b_in[:, 3 * ML_WIDTH:4 * ML_WIDTH], b_in[:, cv0:sq0], b_in[:, sq0:sk0] * (scale * LOG2E), b_in[:, sk0:],
           b_in[:, gi:gi + ML_HEADS], zb(SUBLANES - ML_HEADS),
           b_in[:, gi + ML_HEADS:gi + 2 * ML_HEADS], zb(LANES - SUBLANES - ML_HEADS)]
    b = jnp.concatenate(bp, axis=-1)[:, None, :]
    assert w.shape[-1] == PROJ_COLS and b.shape[-1] == PROJ_COLS
    wt = jnp.swapaxes(w_in[:, :, sk0:], 1, 2).astype(BF16)
    bt = b_in[:, sk0:, None]
    b2 = jnp.stack(_split3(sb_bias * LOG2E), axis=-1)
    bias_rows = jnp.pad(b2, ((0, 0), (0, 0), (HEAD_DIM, LANES - HEAD_DIM - BIAS_LANES)))
    return w, b, wt, bt, bias_rows


def _pad_rows8(a):
    return jnp.pad(a, ((0, 0), (0, SUBLANES - a.shape[1]), (0, 0)))


def _ext_state(c, n):
    ext = jnp.concatenate([c, n[..., None]], axis=-1)
    return jnp.pad(ext, ((0, 0), (0, 0), (0, LANES - HEAD_DIM), (0, LANES - HEAD_DIM - 1)))


def _halo(prev, seq_len):
    bsz, _, c = prev.shape
    z = jnp.zeros((bsz, seq_len, c), F32)
    e1 = z.at[:, 0].set(prev[:, 1])
    e2 = z.at[:, 0].set(prev[:, 0]).at[:, 1].set(prev[:, 1])
    return e1.reshape(bsz * seq_len, c), e2.reshape(bsz * seq_len, c)


def _layer_prompt(x, nseq, wts, layer):
    (w_in, b_in, wt, bt, brows, nw, cw, sbias, w_out, g1, b1, w_up, fcw, w_down, g2, b2) = wts
    m = x.shape[0]
    t = m // nseq
    ml, mo, cv, qb, kb, vb, skt, svt, g = _proj(x, w_in, b_in, wt, bt, layer, ROW_TILE, nseq, True)
    c0 = jnp.zeros((nseq, ML_HEADS, LANES, LANES), F32)
    m0 = jnp.zeros((nseq, 1, LANES), F32)
    hml, c_ext, m_new = _mlstm(ml, mo, g, c0, m0, nw, layer, nseq, MLSTM_CHUNK, MLSTM_CHUNK, MLSTM_SEQS)
    hsb = _sb_prompt(qb, kb, vb, brows, layer, nseq, ATT_TQ, ATT_TK)
    x2, ptail, gtail = _mix_ffn(hml, cv, hsb, x, w_out, cw, g1, b1, w_up, fcw, w_down, g2, b2, layer, ROW_TILE, t)
    last = lambda tail: tail.reshape(nseq, t // ROW_TILE, SUBLANES, -1)[:, -1, SUBLANES - 2:, :]
    states = (skt, svt,
              c_ext[:, :, :HEAD_DIM, :HEAD_DIM], c_ext[:, :, :HEAD_DIM, HEAD_DIM], m_new[:, 0, :ML_HEADS],
              last(ptail), last(gtail))
    return x2, states


def _layer_sample(x, nseq, wts, layer, cache_kt, cache_vt, page_table, c_prev, n_prev, m_prev, conv_prev,
                  ffn_prev):
    (w_in, b_in, wt, bt, brows, nw, cw, sbias, w_out, g1, b1, w_up, fcw, w_down, g2, b2) = wts
    m = x.shape[0]
    t = m // nseq
    ml, mo, cv, qb, _, _, sk, sv, g = _proj(x, w_in, b_in, wt, bt, layer, m, nseq, False)

    pad = lambda a: jnp.pad(a.reshape(nseq, t, -1), ((0, 0), (0, SAMPLE_PAD - t), (0, 0))).reshape(
        nseq * SAMPLE_PAD, -1)
    m0 = jnp.pad(m_prev, ((0, 0), (0, LANES - ML_HEADS)))[:, None, :]
    hml, c_ext, m_new = _mlstm(pad(ml), pad(mo), pad(g), _ext_state(c_prev, n_prev), m0, nw, layer, nseq,
                               SAMPLE_PAD, t, MLSTM_SEQS)
    hml = hml.reshape(nseq, SAMPLE_PAD, ML_WIDTH)[:, :t].reshape(m, ML_WIDTH)

    q3 = qb.reshape(nseq, t, SB_HEADS, HEAD_DIM)
    eye = jnp.eye(SB_HEADS, dtype=BF16)
    qbd = jnp.einsum("bthd,hg->bhtgd", q3, eye).reshape(nseq, SB_HEADS * t, SB_WIDTH)
    bias = jnp.broadcast_to(jnp.repeat(sbias[layer] * LOG2E, t)[:, None], (SB_HEADS * t, PAGE_SIZE))
    new_t = lambda a: jnp.pad(jnp.swapaxes(a.reshape(nseq, t, SB_WIDTH), 1, 2),
                              ((0, 0), (0, 0), (0, PAGE_SIZE - t)))
    hsb = _sb_sample(qbd, bias, new_t(sk), new_t(sv), cache_kt, cache_vt, page_table, layer, t).reshape(
        m, SB_WIDTH)

    x2, p_all, g_all = _mix_ffn(hml, cv, hsb, x, w_out, cw, g1, b1, w_up, fcw, w_down, g2, b2, layer, m, t,
                                _halo(conv_prev, t), _halo(ffn_prev, t))
    last = lambda a: a.reshape(nseq, t, -1)[:, t - 2:, :]
    states = (sk.reshape(nseq, t, SB_HEADS, HEAD_DIM), sv.reshape(nseq, t, SB_HEADS, HEAD_DIM),
              c_ext[:, :, :HEAD_DIM, :HEAD_DIM], c_ext[:, :, :HEAD_DIM, HEAD_DIM], m_new[:, 0, :ML_HEADS],
              last(p_all), last(g_all))
    return x2, states


def _feature_major_pages(cache):
    d, p = cache.shape[:2]
    return jnp.transpose(cache, (0, 1, 3, 4, 2)).reshape(d, p, SB_WIDTH, PAGE_SIZE)


def _token_major_state(kt):
    d, b, _, t = kt.shape
    return jnp.transpose(kt.reshape(d, b, SB_HEADS, HEAD_DIM, t), (0, 1, 4, 2, 3))


def kernel(x_prompt, x_sample, cache_k, cache_v, state_mlstm_c, state_mlstm_n, state_mlstm_m, state_conv,
           state_ffn_conv, page_table, w_in, b_in, mlstm_norm_w, conv_w, sb_bias, w_out, ln1_g, ln1_b, ffn_w_up,
           ffn_conv_w, ffn_w_down, ln2_g, ln2_b):
    bp, tp, _ = x_prompt.shape
    bs, ts, _ = x_sample.shape
    depth = w_in.shape[0]
    w_in_p, b_in_p, wt_p, bt_p, brows = _pack_w_in(w_in, b_in, sb_bias)
    w_out_b = w_out.astype(BF16)
    w_up_b = ffn_w_up.astype(BF16)
    w_down_b = ffn_w_down.astype(BF16)
    cw_p = _pad_rows8(conv_w)
    fcw_p = _pad_rows8(ffn_conv_w)
    ckt = _feature_major_pages(cache_k)
    cvt = _feature_major_pages(cache_v)

    yp = x_prompt.reshape(bp * tp, D_MODEL)
    ys = x_sample.reshape(bs * ts, D_MODEL)
    st_p = [[] for _ in range(7)]
    st_s = [[] for _ in range(7)]
    row3 = lambda a: a[:, None, :]
    wts = (w_in_p, b_in_p, wt_p, bt_p, brows, row3(mlstm_norm_w), cw_p, sb_bias, w_out_b, row3(ln1_g), row3(ln1_b),
           w_up_b, fcw_p, w_down_b, row3(ln2_g), row3(ln2_b))
    for l in range(depth):
        yp, new_p = _layer_prompt(yp, bp, wts, l)
        ys, new_s = _layer_sample(ys, bs, wts, l, ckt, cvt, page_table, state_mlstm_c[l], state_mlstm_n[l],
                                  state_mlstm_m[l], state_conv[l], state_ffn_conv[l])
        for lst, a in zip(st_p, new_p):
            lst.append(a)
        for lst, a in zip(st_s, new_s):
            lst.append(a)
    outs_p = [jnp.stack(s) for s in st_p]
    outs_p[0] = _token_major_state(outs_p[0])
    outs_p[1] = _token_major_state(outs_p[1])
    outs_s = [jnp.stack(s) for s in st_s]
    return (yp.reshape(bp, tp, D_MODEL), ys.reshape(bs, ts, D_MODEL), *outs_p, *outs_s)
```

```python
import functools

import jax
import jax.numpy as jnp
from jax import lax
from jax.experimental import pallas as pl
from jax.experimental.pallas import tpu as pltpu

F32 = jnp.float32
BF16 = jnp.bfloat16

D_MODEL = 1024
DEPTH = 4
ML_HEADS = 4
HEAD_DIM = 64
ML_WIDTH = ML_HEADS * HEAD_DIM
CONV_DIM = 256
CONV_WIDTH = 3
SB_HEADS = 8
SB_WIDTH = SB_HEADS * HEAD_DIM
D_FF = 2816
PAGE_SIZE = 128
LN_EPS = 1e-5
DN_ALPHA = (2 * DEPTH) ** 0.25

LANES = 128
SUBLANES = 8
MIB = 1024 * 1024

C_ML = 0
ML_COLS = ML_HEADS * 3 * LANES
C_MO = C_ML + ML_COLS
C_CV = C_MO + ML_WIDTH
C_SQ = C_CV + 3 * CONV_DIM
C_SK = C_SQ + SB_WIDTH
C_SV = C_SK + SB_WIDTH
C_G = C_SV + SB_WIDTH
PROJ_COLS = C_G + LANES
LOG2E = 1.4426950408889634
BIAS_LANES = 3

MLSTM_CHUNK = 512
MLSTM_SEQS = 1
SAMPLE_PAD = 128
ROW_TILE = 512
ATT_TQ = 512
ATT_TK = 256
PAGES_PER_STEP = 16
FF_CHUNK = 256


def _cparams(sem, vmem_mib):
    return pltpu.CompilerParams(dimension_semantics=sem, vmem_limit_bytes=vmem_mib * MIB)


def _const_spec(shape):
    zeros = (0,) * len(shape)
    return pl.BlockSpec(shape, lambda *_: zeros)


def _layer_spec(shape, layer, **kwargs):
    index = (layer,) + (0,) * len(shape)
    return pl.BlockSpec((None,) + tuple(shape), lambda *_: index, **kwargs)


def _softplus(z):
    return jnp.maximum(z, 0.0) + jnp.log(1.0 + jnp.exp(-jnp.abs(z)))


def _softplus2(z):
    sign_bit = jnp.uint32(0x80000000)
    neg_abs = lax.bitcast_convert_type(lax.bitcast_convert_type(z, jnp.uint32) | sign_bit, F32)
    return jnp.maximum(z, 0.0) + jnp.log2(1.0 + jnp.exp2(neg_abs))


def _layer_norm(y, g, b):
    mu = jnp.mean(y, axis=-1, keepdims=True)
    d = y - mu
    var = jnp.mean(d * d, axis=-1, keepdims=True)
    return d * lax.rsqrt(var + LN_EPS) * g + b


def _proj_kernel(x_ref, w_ref, b_ref, wt_ref, bt_ref, *refs, kv_transposed):
    ml_ref, mo_ref, cv_ref, qb_ref, kb_ref, vb_ref, k_ref, v_ref, g_ref = refs[2:] if kv_transposed else refs
    x = x_ref[...].astype(BF16)

    def mm(c0, c1):
        return jnp.dot(x, w_ref[:, c0:c1], preferred_element_type=F32) + b_ref[:, c0:c1]

    for c in range(0, ML_COLS, 512):
        ml_ref[:, c:c + 512] = mm(C_ML + c, C_ML + c + 512).astype(BF16)
    mo_ref[...] = mm(C_MO, C_MO + ML_WIDTH)
    for c in range(0, 3 * CONV_DIM, CONV_DIM):
        cv_ref[:, c:c + CONV_DIM] = mm(C_CV + c, C_CV + c + CONV_DIM)
    qb_ref[...] = mm(C_SQ, C_SQ + SB_WIDTH).astype(BF16)
    k = mm(C_SK, C_SK + SB_WIDTH)
    kb_ref[...] = k.astype(BF16)
    v = mm(C_SV, C_SV + SB_WIDTH)
    vb_ref[...] = v.astype(BF16)
    if kv_transposed:
        for r, ref in ((0, k_ref), (SB_WIDTH, v_ref)):
            ref[...] = lax.dot_general(wt_ref[r:r + SB_WIDTH, :], x, (((1,), (1,)), ((), ())),
                                       preferred_element_type=F32) + bt_ref[r:r + SB_WIDTH, :]
    else:
        k_ref[...] = k
        v_ref[...] = v
    g_ref[...] = mm(C_G, C_G + LANES)


def _proj(x, w, b, wt, bt, layer, tm, nseq, kv_state=None):
    m = x.shape[0]
    t = m // nseq
    kv_transposed = kv_state is not None
    row = lambda width: pl.BlockSpec((tm, width), lambda i: (i, 0))
    outs = [(ML_COLS, BF16), (ML_WIDTH, F32), (3 * CONV_DIM, F32), (SB_WIDTH, BF16), (SB_WIDTH, BF16),
            (SB_WIDTH, BF16), (SB_WIDTH, F32), (SB_WIDTH, F32), (LANES, F32)]
    out_specs = [row(wd) for wd, _ in outs]
    out_shape = [jax.ShapeDtypeStruct((m, wd), dt) for wd, dt in outs]
    in_specs = [row(D_MODEL),
                _layer_spec((D_MODEL, PROJ_COLS), layer, pipeline_mode=pl.Buffered(1)),
                _layer_spec((1, PROJ_COLS), layer),
                _layer_spec((2 * SB_WIDTH, D_MODEL), layer), _layer_spec((2 * SB_WIDTH, 1), layer)]
    args = [x, w, b, wt, bt]
    aliases = {}
    if kv_transposed:
        tiles = t // tm
        for n, idx in enumerate((6, 7)):
            out_specs[idx] = pl.BlockSpec((None, None, SB_WIDTH, tm), lambda i: (layer, i // tiles, 0, i % tiles))
            out_shape[idx] = jax.ShapeDtypeStruct(kv_state[n].shape, F32)
            in_specs.append(pl.BlockSpec(memory_space=pl.ANY))
            aliases[len(args)] = idx
            args.append(kv_state[n])
    return pl.pallas_call(
        functools.partial(_proj_kernel, kv_transposed=kv_transposed),
        grid=(m // tm,),
        in_specs=in_specs,
        out_specs=out_specs,
        out_shape=out_shape,
        input_output_aliases=aliases,
        compiler_params=_cparams(("arbitrary",), 56),
        name="proj",
    )(*args)


def _mlstm_kernel(ml_ref, mo_ref, g_ref, c0_ref, m0_ref, nw_ref, up_ref, h_ref, c_ref, m_ref, cst, mst,
                  *, chunk, valid_len, nb):
    L = chunk

    @pl.when(pl.program_id(1) == 0)
    def _():
        cst[...] = c0_ref[...]
        mst[...] = m0_ref[...]

    up = up_ref[...]
    row_i = lax.broadcasted_iota(jnp.int32, (L, L), 0)
    col_i = lax.broadcasted_iota(jnp.int32, (L, L), 1)
    causal = row_i >= col_i
    lane_m = lax.broadcasted_iota(jnp.int32, (1, LANES), 1)

    a_rows, cols_all, m_prev_all = [], [], []
    for s in range(nb):
        gt = g_ref[s].T
        t_ig = gt[0:SUBLANES, :]
        t_f = gt[SUBLANES:2 * SUBLANES, :]
        lf = -_softplus(-t_f)
        if valid_len < L:
            pos = lax.broadcasted_iota(jnp.int32, (SUBLANES, L), 1)
            t_ig = jnp.where(pos < valid_len, t_ig, -1e30)
            lf = jnp.where(pos < valid_len, lf, 0.0)
        hi = lf.astype(BF16)
        r1 = lf - hi.astype(F32)
        mid = r1.astype(BF16)
        lo = (r1 - mid.astype(F32)).astype(BF16)
        b_row = (jnp.dot(hi, up, preferred_element_type=F32) + jnp.dot(mid, up, preferred_element_type=F32)
                 + jnp.dot(lo, up, preferred_element_type=F32))
        a_row = t_ig - b_row
        a_rows.append(a_row)
        pos_l = lax.broadcasted_iota(jnp.int32, (SUBLANES, L), 1)
        cmax = a_row
        shift = 1
        while shift < L:
            cmax = jnp.maximum(cmax, jnp.where(pos_l >= shift, pltpu.roll(cmax, shift, axis=1), -jnp.inf))
            shift *= 2
        cols_all.append(jnp.concatenate([a_row, b_row, cmax, jnp.zeros((LANES - 3 * SUBLANES, L), F32)],
                                        axis=0).T)
        m_prev_all.append(mst[s])

    chains = [(s, h) for s in range(nb) for h in range(ML_HEADS)]
    grp = lambda h, part: slice(3 * LANES * h + part * LANES, 3 * LANES * h + (part + 1) * LANES)
    qs = [ml_ref[s, :, grp(h, 0)] for s, h in chains]
    ks = [ml_ref[s, :, grp(h, 1)] for s, h in chains]
    vs = [ml_ref[s, :, grp(h, 2)] for s, h in chains]
    c_old = [cst[s, h] for s, h in chains]
    s_all = [lax.dot_general(q, k, (((1,), (1,)), ((), ())), preferred_element_type=F32) for q, k in zip(qs, ks)]
    qc_all = [jnp.dot(q, c.astype(BF16), preferred_element_type=F32) for q, c in zip(qs, c_old)]

    w_all, kw_all, inter_all, mt_all, decay_all = [], [], [], [], []
    m_next = list(m_prev_all)
    for c, (s, h) in enumerate(chains):
        a_c = cols_all[s][:, h:h + 1]
        b_c = cols_all[s][:, SUBLANES + h:SUBLANES + h + 1]
        a_r = a_rows[s][h:h + 1, :]
        m_prev = m_prev_all[s][:, h:h + 1]
        g_c = jnp.maximum(cols_all[s][:, 2 * SUBLANES + h:2 * SUBLANES + h + 1], m_prev)
        d = jnp.where(causal, jnp.exp(a_r - g_c), 0.0)
        inter_all.append(jnp.exp(m_prev - g_c))
        mt_all.append(b_c + g_c)
        w_all.append((s_all[c] * d).astype(BF16))
        g_last = g_c[L - 1:L, :]
        m_new = b_c[L - 1:L, :] + g_last
        w_s = jnp.exp(a_c - g_last)
        decay_all.append(jnp.exp(m_prev - g_last))
        kw_all.append((ks[c].astype(F32) * w_s).astype(BF16))
        m_next[s] = jnp.where(lane_m == h, m_new, m_next[s])

    wv_all = [jnp.dot(w, v, preferred_element_type=F32) for w, v in zip(w_all, vs)]
    upd_all = [lax.dot_general(kw, v, (((0,), (0,)), ((), ())), preferred_element_type=F32)
               for kw, v in zip(kw_all, vs)]

    jmat = jnp.where(lax.broadcasted_iota(jnp.int32, (LANES, LANES), 0) < HEAD_DIM, 1.0 / HEAD_DIM, 0.0).astype(BF16)

    def lane_mean(x):
        hi = x.astype(BF16)
        lo = (x - hi.astype(F32)).astype(BF16)
        return jnp.dot(hi, jmat, preferred_element_type=F32) + jnp.dot(lo, jmat, preferred_element_type=F32)

    hh_all = []
    for c, (s, h) in enumerate(chains):
        nd = wv_all[c] + inter_all[c] * qc_all[c]
        den = nd[:, HEAD_DIM:HEAD_DIM + 1]
        hh_all.append(nd / jnp.maximum(jnp.abs(den), jnp.exp(-mt_all[c])))
    mu_all = [lane_mean(hh) for hh in hh_all]
    dh_all = [hh - mu for hh, mu in zip(hh_all, mu_all)]
    var_all = [lane_mean(dh * dh) for dh in dh_all]
    for c, (s, h) in enumerate(chains):
        o = mo_ref[s, :, HEAD_DIM * h:HEAD_DIM * (h + 1)]
        gate = 1.0 / (1.0 + jnp.exp(-o))
        hn = (dh_all[c] * lax.rsqrt(var_all[c] + LN_EPS))[:, 0:HEAD_DIM] * nw_ref[:, HEAD_DIM * h:HEAD_DIM * (h + 1)]
        h_ref[s, :, HEAD_DIM * h:HEAD_DIM * (h + 1)] = (hn * gate).astype(BF16)
        cst[s, h] = decay_all[c] * c_old[c] + upd_all[c]

    for s in range(nb):
        mst[s] = m_next[s]
        m_ref[s] = m_next[s]
    c_ref[...] = cst[...]


def _mlstm(ml, mo, g, c0, m0, nw, layer, nseq, chunk, valid_len, nb):
    m = ml.shape[0]
    t = m // nseq
    nc = t // chunk
    up = (lax.broadcasted_iota(jnp.int32, (chunk, chunk), 0)
          <= lax.broadcasted_iota(jnp.int32, (chunk, chunk), 1)).astype(BF16)
    row = lambda width: pl.BlockSpec((nb, chunk, width), lambda b, j: (b, j, 0))
    st_c = pl.BlockSpec((nb, ML_HEADS, LANES, LANES), lambda b, j: (b, 0, 0, 0))
    st_m = pl.BlockSpec((nb, 1, LANES), lambda b, j: (b, 0, 0))
    seq3 = lambda a: a.reshape(nseq, t, a.shape[-1])
    h, c_new, m_new = pl.pallas_call(
        functools.partial(_mlstm_kernel, chunk=chunk, valid_len=valid_len, nb=nb),
        grid=(nseq // nb, nc),
        in_specs=[row(ML_COLS), row(ML_WIDTH), row(LANES), st_c, st_m, _layer_spec((1, ML_WIDTH), layer),
                  _const_spec((chunk, chunk))],
        out_specs=[row(ML_WIDTH), st_c, st_m],
        out_shape=[jax.ShapeDtypeStruct((nseq, t, ML_WIDTH), BF16),
                   jax.ShapeDtypeStruct((nseq, ML_HEADS, LANES, LANES), F32),
                   jax.ShapeDtypeStruct((nseq, 1, LANES), F32)],
        scratch_shapes=[pltpu.VMEM((nb, ML_HEADS, LANES, LANES), F32), pltpu.VMEM((nb, 1, LANES), F32)],
        compiler_params=_cparams(("arbitrary", "arbitrary"), 40),
        name="mlstm",
    )(seq3(ml), seq3(mo), seq3(g), c0, m0, nw, up)
    return h.reshape(m, ML_WIDTH), c_new, m_new


def _sb_prompt_kernel(qp_ref, kp_ref, v_ref, brow_ref, tri_ref, o_ref, acc_ref, carry_ref, q_ref, k_ref, *, tq, tk):
    i = pl.program_id(2)
    nkb = tq // tk
    tri = tri_ref[...]
    acc_ref[...] = jnp.zeros_like(acc_ref)
    carry_ref[...] = jnp.zeros_like(carry_ref)

    def head_groups(pair, extra):
        x = pair.astype(F32)
        lane = lax.broadcasted_iota(jnp.int32, x.shape, 1)
        return [jnp.where(lane < HEAD_DIM, x if hh == 0 else pltpu.roll(x, HEAD_DIM, axis=1), extra[hh]).astype(BF16)
                for hh in range(2)]

    @pl.when(i == 0)
    def _():
        lane = lax.broadcasted_iota(jnp.int32, (tq, LANES), 1)
        ones = jnp.where((lane >= HEAD_DIM) & (lane < HEAD_DIM + BIAS_LANES), 1.0, 0.0)

        def fill(r, _):
            rows = pl.ds(pl.multiple_of(r * tq, tq), tq)
            for hh, grp in enumerate(head_groups(kp_ref[rows, :], [ones, ones])):
                k_ref[rows, hh * LANES:(hh + 1) * LANES] = grp
            return 0

        lax.fori_loop(0, kp_ref.shape[0] // tq, fill, 0)

    for hh, grp in enumerate(head_groups(qp_ref[...], [brow_ref[0:1, :], brow_ref[1:2, :]])):
        q_ref[:, hh * LANES:(hh + 1) * LANES] = grp

    def steps(items):
        chains = [(n, hh) for n in range(len(items)) for hh in range(2)]
        starts = [pl.multiple_of(j * tk, tk) for j, _, _, _ in items]
        keep = []
        for j, r0, nr, masked in items:
            if masked:
                qpos = i * tq + r0 + lax.broadcasted_iota(jnp.int32, (nr, tk), 0)
                kpos = j * tk + lax.broadcasted_iota(jnp.int32, (nr, tk), 1)
                keep.append(kpos < qpos)
            else:
                keep.append(None)
        z = [lax.dot_general(q_ref[items[n][1]:items[n][1] + items[n][2], hh * LANES:(hh + 1) * LANES],
                             k_ref[pl.ds(starts[n], tk), hh * LANES:(hh + 1) * LANES],
                             (((1,), (1,)), ((), ())), preferred_element_type=F32) for n, hh in chains]
        cum = []
        for c, (n, hh) in enumerate(chains):
            sp = _softplus2(z[c])
            if keep[n] is not None:
                sp = jnp.where(keep[n], sp, 0.0)
            cum.append(jnp.dot(sp.astype(BF16), tri, preferred_element_type=F32))
        pv = []
        for c, (n, hh) in enumerate(chains):
            a = jnp.exp2(z[c] + cum[c])
            if keep[n] is not None:
                a = jnp.where(keep[n], a, 0.0)
            pv.append(jnp.dot(a.astype(BF16), v_ref[pl.ds(starts[n], tk), :], preferred_element_type=F32))
        for c, (n, hh) in enumerate(chains):
            rows = slice(items[n][1], items[n][1] + items[n][2])
            carry = carry_ref[hh, rows]
            acc_ref[hh, rows] += jnp.exp2(carry) * pv[c]
            carry_ref[hh, rows] = carry + cum[c][:, 0:1]

    assert nkb == 2
    steps([(i * nkb + 1, tk, tk, True), (i * nkb, 0, tk, True), (i * nkb, tk, tk, False)])
    full = lambda j: (j, 0, tq, False)

    top = i * nkb - 1
    one = i % 2

    @pl.when(one == 1)
    def _():
        steps([full(top - n) for n in range(2)])

    def body(t, _):
        j0 = top - 2 * one - 4 * t
        steps([full(j0 - n) for n in range(4)])
        return 0

    lax.fori_loop(0, i // 2, body, 0)

    lane = lax.broadcasted_iota(jnp.int32, (tq, LANES), 1)
    o_ref[...] = jnp.where(lane >= HEAD_DIM, acc_ref[1], acc_ref[0]).astype(BF16)


def _sb_prompt(qb, kb, vb, bias_rows, layer, nseq, tq, tk):
    m = qb.shape[0]
    t = m // nseq
    nq = t // tq
    tri = -(lax.broadcasted_iota(jnp.int32, (tk, tk), 0)
            >= lax.broadcasted_iota(jnp.int32, (tk, tk), 1)).astype(BF16)
    pair_rows = bias_rows.reshape(bias_rows.shape[0], SB_HEADS // 2, 2, LANES)
    return pl.pallas_call(
        functools.partial(_sb_prompt_kernel, tq=tq, tk=tk),
        grid=(nseq, SB_HEADS // 2, nq),
        in_specs=[pl.BlockSpec((tq, LANES), lambda b, hp, i: (b * nq + i, hp)),
                  pl.BlockSpec((t, LANES), lambda b, hp, i: (b, hp)),
                  pl.BlockSpec((t, LANES), lambda b, hp, i: (b, hp)),
                  pl.BlockSpec((None, None, 2, LANES), lambda b, hp, i: (layer, hp, 0, 0)),
                  pl.BlockSpec((tk, tk), lambda b, hp, i: (0, 0))],
        out_specs=pl.BlockSpec((tq, LANES), lambda b, hp, i: (b * nq + i, hp)),
        out_shape=jax.ShapeDtypeStruct((m, SB_WIDTH), BF16),
        scratch_shapes=[pltpu.VMEM((2, tq, LANES), F32), pltpu.VMEM((2, tq, 1), F32),
                        pltpu.VMEM((tq, 2 * LANES), BF16), pltpu.VMEM((t, 2 * LANES), BF16)],
        compiler_params=_cparams(("arbitrary", "arbitrary", "arbitrary"), 40),
        name="sb_prompt",
    )(qb, kb, vb, pair_rows, tri)


def _sb_sample_kernel(pt_ref, qbd_ref, bias_ref, kn_ref, vn_ref, tri_ref, *refs, pps, nq):
    k_refs = refs[0:pps]
    v_refs = refs[pps:2 * pps]
    o_ref = refs[2 * pps]
    acc_ref, carry_ref = refs[2 * pps + 1:]
    j = pl.program_id(1)
    qbd = qbd_ref[0]
    bias = bias_ref[...]
    tri = tri_ref[...]

    def blocks(kts, vts, keep):
        z = [jnp.dot(qbd, kt[...].astype(BF16), preferred_element_type=F32) + bias for kt in kts]
        cum = []
        for zc in z:
            sp = _softplus2(zc)
            if keep is not None:
                sp = jnp.where(keep, sp, 0.0)
            cum.append(jnp.dot(sp.astype(BF16), tri, preferred_element_type=F32))
        pv = []
        for zc, cc, vt in zip(z, cum, vts):
            a = jnp.exp2(zc + cc)
            if keep is not None:
                a = jnp.where(keep, a, 0.0)
            pv.append(lax.dot_general(a.astype(BF16), vt[...].astype(BF16), (((1,), (1,)), ((), ())),
                                      preferred_element_type=F32))
        acc = acc_ref[...]
        carry = carry_ref[...]
        for pc, cc in zip(pv, cum):
            acc = acc + jnp.exp2(carry) * pc
            carry = carry + cc[:, 0:1]
        acc_ref[...] = acc
        carry_ref[...] = carry

    @pl.when(j == 0)
    def _():
        acc_ref[...] = jnp.zeros_like(acc_ref)
        carry_ref[...] = jnp.zeros_like(carry_ref)
        t_i = lax.broadcasted_iota(jnp.int32, (SB_HEADS * nq, PAGE_SIZE), 0) % nq
        s_i = lax.broadcasted_iota(jnp.int32, (SB_HEADS * nq, PAGE_SIZE), 1)
        blocks([kn_ref.at[0]], [vn_ref.at[0]], s_i < t_i)

    blocks(k_refs, v_refs, None)

    @pl.when(j == pl.num_programs(1) - 1)
    def _():
        acc = acc_ref[...]
        lane_h = lax.broadcasted_iota(jnp.int32, (nq, SB_WIDTH), 1) // HEAD_DIM
        res = jnp.zeros((nq, SB_WIDTH), F32)
        for h in range(SB_HEADS):
            res = jnp.where(lane_h == h, acc[nq * h:nq * (h + 1), :], res)
        o_ref[0] = res


def _sb_sample(qbd, bias, kt_new, vt_new, cache_kt, cache_vt, page_table, layer, nq):
    bsz, n_pages = page_table.shape
    pps = PAGES_PER_STEP
    steps = n_pages // pps
    rows = SB_HEADS * nq
    tri = -(lax.broadcasted_iota(jnp.int32, (PAGE_SIZE, PAGE_SIZE), 0)
            >= lax.broadcasted_iota(jnp.int32, (PAGE_SIZE, PAGE_SIZE), 1)).astype(BF16)

    def page_spec(p):
        def imap(b, j, pt):
            return (layer, pt[b * n_pages + (n_pages - 1 - (j * pps + p))], 0, 0)
        return pl.BlockSpec((None, None, SB_WIDTH, PAGE_SIZE), imap)

    per_seq = lambda shape: pl.BlockSpec((1,) + shape, lambda b, j, pt: (b, 0, 0))
    grid_spec = pltpu.PrefetchScalarGridSpec(
        num_scalar_prefetch=1,
        grid=(bsz, steps),
        in_specs=[per_seq((rows, SB_WIDTH)), pl.BlockSpec((rows, PAGE_SIZE), lambda b, j, pt: (0, 0)),
                  per_seq((SB_WIDTH, PAGE_SIZE)), per_seq((SB_WIDTH, PAGE_SIZE)),
                  pl.BlockSpec((PAGE_SIZE, PAGE_SIZE), lambda b, j, pt: (0, 0))]
                 + [page_spec(p) for p in range(pps)] + [page_spec(p) for p in range(pps)],
        out_specs=per_seq((nq, SB_WIDTH)),
        scratch_shapes=[pltpu.VMEM((rows, SB_WIDTH), F32), pltpu.VMEM((rows, 1), F32)],
    )
    return pl.pallas_call(
        functools.partial(_sb_sample_kernel, pps=pps, nq=nq),
        grid_spec=grid_spec,
        out_shape=jax.ShapeDtypeStruct((bsz, nq, SB_WIDTH), F32),
        compiler_params=_cparams(("arbitrary", "arbitrary"), 40),
        name="sb_sample",
    )(page_table.reshape(-1), qbd, bias, kt_new, vt_new, tri, *([cache_kt] * pps), *([cache_vt] * pps))


def _conv3(p, s_ref, cs, w_ref, tm, halo):
    s_ref[SUBLANES:SUBLANES + tm, cs] = p
    p1 = s_ref[SUBLANES - 1:SUBLANES - 1 + tm, cs]
    p2 = s_ref[SUBLANES - 2:SUBLANES - 2 + tm, cs]
    if halo is not None:
        e1, e2, seq_len = halo
        tpos = lax.broadcasted_iota(jnp.int32, p.shape, 0) % seq_len
        p1 = jnp.where(tpos == 0, e1, p1)
        p2 = jnp.where(tpos < 2, e2, p2)
    return w_ref[0:1, cs] * p2 + w_ref[1:2, cs] * p1 + w_ref[2:3, cs] * p


def _start_tile(s_ref, tiles_per_seq):
    @pl.when(pl.program_id(0) % tiles_per_seq == 0)
    def _():
        s_ref[0:SUBLANES, :] = jnp.zeros((SUBLANES, s_ref.shape[1]), F32)


def _carry_rows(s_ref, tm):
    s_ref[0:SUBLANES, :] = s_ref[tm:tm + SUBLANES, :]


def _mix_rows(hml_ref, cv_ref, hsb_ref, x_ref, w_ref, cw_ref, g_ref, b_ref, tail_ref, s_ref, halo_refs, tm, seq_len):
    sample = halo_refs is not None
    halo = (halo_refs[0][...], halo_refs[1][...], seq_len) if sample else None
    cb = cv_ref[:, 0:CONV_DIM]
    p = cv_ref[:, CONV_DIM:2 * CONV_DIM] * cv_ref[:, 2 * CONV_DIM:3 * CONV_DIM]
    u = _conv3(p, s_ref, slice(None), cw_ref, tm, halo)
    if sample:
        tail_ref[...] = p
    else:
        tail_ref[...] = p[tm - SUBLANES:tm, :]
        _carry_rows(s_ref, tm)
    h_conv = (cb * u).astype(BF16)
    mix = (jnp.dot(hml_ref[...], w_ref[0:ML_WIDTH, :], preferred_element_type=F32)
           + jnp.dot(h_conv, w_ref[ML_WIDTH:ML_WIDTH + CONV_DIM, :], preferred_element_type=F32)
           + jnp.dot(hsb_ref[...].astype(BF16), w_ref[ML_WIDTH + CONV_DIM:, :], preferred_element_type=F32))
    return _layer_norm(DN_ALPHA * x_ref[...] + mix, g_ref[...], b_ref[...])


def _ffn_rows(x, wu_ref, cw_ref, wd_ref, g_ref, b_ref, tail_ref, s_ref, hid_ref, halo_refs, tm, seq_len):
    sample = halo_refs is not None
    if sample:
        e1_ref, e2_ref = halo_refs
    xb = x.astype(BF16)
    n_chunks = D_FF // FF_CHUNK

    def up(c):
        return (jnp.dot(xb, wu_ref[:, c * FF_CHUNK:(c + 1) * FF_CHUNK], preferred_element_type=F32),
                jnp.dot(xb, wu_ref[:, D_FF + c * FF_CHUNK:D_FF + (c + 1) * FF_CHUNK], preferred_element_type=F32))

    nxt = up(0)
    for c in range(n_chunks):
        cs = slice(c * FF_CHUNK, (c + 1) * FF_CHUNK)
        g_pre, val = nxt
        if c + 1 < n_chunks:
            nxt = up(c + 1)
        halo = (e1_ref[:, cs], e2_ref[:, cs], seq_len) if sample else None
        g_conv = _conv3(g_pre, s_ref, cs, cw_ref, tm, halo)
        hid_ref[:, cs] = (g_conv / (1.0 + jnp.exp(-g_conv)) * val).astype(BF16)
    if sample:
        tail_ref[...] = s_ref[SUBLANES:SUBLANES + tm, :]
    else:
        tail_ref[...] = s_ref[tm:tm + SUBLANES, :]
        _carry_rows(s_ref, tm)
    ff = jnp.dot(hid_ref[...], wd_ref[...], preferred_element_type=F32)
    return _layer_norm(DN_ALPHA * x + ff, g_ref[...], b_ref[...])


def _mix_ffn_kernel(*refs, tm, tiles_per_seq, seq_len):
    sample = seq_len < tm
    (hml_ref, cv_ref, hsb_ref, x_ref, wo_ref, cw_ref, g1_ref, b1_ref, wu_ref, fcw_ref, wd_ref, g2_ref,
     b2_ref) = refs[:13]
    if sample:
        e1c_ref, e2c_ref, e1f_ref, e2f_ref, y_ref, ctail_ref, ftail_ref, sc_ref, sf_ref, hid_ref = refs[13:]
        halo_c, halo_f = (e1c_ref, e2c_ref), (e1f_ref, e2f_ref)
    else:
        y_ref, ctail_ref, ftail_ref, sc_ref, sf_ref, hid_ref = refs[13:]
        halo_c = halo_f = None
    _start_tile(sc_ref, tiles_per_seq)
    _start_tile(sf_ref, tiles_per_seq)
    x1 = _mix_rows(hml_ref, cv_ref, hsb_ref, x_ref, wo_ref, cw_ref, g1_ref, b1_ref, ctail_ref, sc_ref, halo_c, tm,
                   seq_len)
    y_ref[...] = _ffn_rows(x1, wu_ref, fcw_ref, wd_ref, g2_ref, b2_ref, ftail_ref, sf_ref, hid_ref, halo_f, tm,
                           seq_len)


def _mix_ffn(hml, cv, hsb, x, wo, cw, g1, b1, wu, fcw, wd, g2, b2, layer, tm, seq_len, halo_c=None, halo_f=None):
    m = x.shape[0]
    sample = seq_len < tm
    tiles_per_seq = max(seq_len // tm, 1)
    row = lambda width: pl.BlockSpec((tm, width), lambda i: (i, 0))
    tail_rows = tm if sample else SUBLANES
    tail = lambda width: pl.BlockSpec((tail_rows, width), lambda i: (i, 0))
    in_specs = [row(ML_WIDTH), row(3 * CONV_DIM), row(SB_WIDTH), row(D_MODEL),
                _layer_spec((D_MODEL, D_MODEL), layer, pipeline_mode=pl.Buffered(1)),
                _layer_spec((SUBLANES, CONV_DIM), layer),
                _layer_spec((1, D_MODEL), layer), _layer_spec((1, D_MODEL), layer),
                _layer_spec((D_MODEL, 2 * D_FF), layer, pipeline_mode=pl.Buffered(1)),
                _layer_spec((SUBLANES, D_FF), layer),
                _layer_spec((D_FF, D_MODEL), layer, pipeline_mode=pl.Buffered(1)),
                _layer_spec((1, D_MODEL), layer), _layer_spec((1, D_MODEL), layer)]
    args = [hml, cv, hsb, x, wo, cw, g1, b1, wu, fcw, wd, g2, b2]
    if sample:
        in_specs += [row(CONV_DIM), row(CONV_DIM), row(D_FF), row(D_FF)]
        args += list(halo_c) + list(halo_f)
    return pl.pallas_call(
        functools.partial(_mix_ffn_kernel, tm=tm, tiles_per_seq=tiles_per_seq, seq_len=seq_len),
        grid=(m // tm,),
        in_specs=in_specs,
        out_specs=[row(D_MODEL), tail(CONV_DIM), tail(D_FF)],
        out_shape=[jax.ShapeDtypeStruct((m, D_MODEL), F32),
                   jax.ShapeDtypeStruct((m // tm * tail_rows, CONV_DIM), F32),
                   jax.ShapeDtypeStruct((m // tm * tail_rows, D_FF), F32)],
        scratch_shapes=[pltpu.VMEM((tm + SUBLANES, CONV_DIM), F32), pltpu.VMEM((tm + SUBLANES, D_FF), F32),
                        pltpu.VMEM((tm, D_FF), BF16)],
        compiler_params=_cparams(("arbitrary",), 56),
        name="mix_ffn",
    )(*args)


def _split3(x):
    hi = x.astype(BF16).astype(F32)
    mid = (x - hi).astype(BF16).astype(F32)
    lo = (x - hi - mid).astype(BF16).astype(F32)
    return hi, mid, lo


def _pack_w_in(w_in, b_in, sb_bias):
    depth = w_in.shape[0]
    scale = HEAD_DIM ** -0.5
    zb = lambda n: jnp.zeros((depth, n), F32)
    gi = 4 * ML_WIDTH
    cv0 = gi + 2 * ML_HEADS
    sq0 = cv0 + 3 * CONV_DIM
    sk0 = sq0 + SB_WIDTH

    def head_groups(lo, nheads, factor=None):
        seg = w_in[:, :, lo:lo + HEAD_DIM * nheads].reshape(depth, D_MODEL, nheads, HEAD_DIM)
        if factor is not None:
            seg = seg * factor
        return jnp.pad(seg, ((0, 0), (0, 0), (0, 0), (0, LANES - HEAD_DIM)))

    ml = jnp.stack([head_groups(0, ML_HEADS), head_groups(ML_WIDTH, ML_HEADS, scale),
                    head_groups(2 * ML_WIDTH, ML_HEADS)], axis=3).reshape(depth, D_MODEL, ML_COLS)
    gates = jnp.pad(w_in[:, :, gi:gi + 2 * ML_HEADS].reshape(depth, D_MODEL, 2, ML_HEADS),
                    ((0, 0), (0, 0), (0, 0), (0, SUBLANES - ML_HEADS))).reshape(depth, D_MODEL, 2 * SUBLANES)
    gates = jnp.pad(gates, ((0, 0), (0, 0), (0, LANES - 2 * SUBLANES)))
    w = jnp.concatenate([ml, w_in[:, :, 3 * ML_WIDTH:4 * ML_WIDTH], w_in[:, :, cv0:sq0],
                         w_in[:, :, sq0:sk0] * (scale * LOG2E), w_in[:, :, sk0:], gates], axis=-1).astype(BF16)

    bp = []
    for h in range(ML_HEADS):
        lo, hi = HEAD_DIM * h, HEAD_DIM * (h + 1)
        bp += [b_in[:, lo:hi], zb(HEAD_DIM)]
        bp += [b_in[:, ML_WIDTH + lo:ML_WIDTH + hi] * scale, zb(HEAD_DIM)]
        bp += [b_in[:, 2 * ML_WIDTH + lo:2 * ML_WIDTH + hi], jnp.ones((depth, 1), F32), zb(HEAD_DIM - 1)]
    bp += [b_in[:, 3 * ML_WIDTH:4 * ML_WIDTH], b_in[:, cv0:sq0], b_in[:, sq0:sk0] * (scale * LOG2E), b_in[:, sk0:],
           b_in[:, gi:gi + ML_HEADS], zb(SUBLANES - ML_HEADS),
           b_in[:, gi + ML_HEADS:gi + 2 * ML_HEADS], zb(LANES - SUBLANES - ML_HEADS)]
    b = jnp.concatenate(bp, axis=-1)[:, None, :]
    assert w.shape[-1] == PROJ_COLS and b.shape[-1] == PROJ_COLS
    wt = jnp.swapaxes(w_in[:, :, sk0:], 1, 2).astype(BF16)
    bt = b_in[:, sk0:, None]
    b2 = jnp.stack(_split3(sb_bias * LOG2E), axis=-1)
    bias_rows = jnp.pad(b2, ((0, 0), (0, 0), (HEAD_DIM, LANES - HEAD_DIM - BIAS_LANES)))
    return w, b, wt, bt, bias_rows


def _pad_rows8(a):
    return jnp.pad(a, ((0, 0), (0, SUBLANES - a.shape[1]), (0, 0)))


def _ext_state(c, n):
    ext = jnp.concatenate([c, n[..., None]], axis=-1)
    return jnp.pad(ext, ((0, 0), (0, 0), (0, LANES - HEAD_DIM), (0, LANES - HEAD_DIM - 1)))


def _halo(prev, seq_len):
    bsz, _, c = prev.shape
    z = jnp.zeros((bsz, seq_len, c), F32)
    e1 = z.at[:, 0].set(prev[:, 1])
    e2 = z.at[:, 0].set(prev[:, 0]).at[:, 1].set(prev[:, 1])
    return e1.reshape(bsz * seq_len, c), e2.reshape(bsz * seq_len, c)


def _layer_prompt(x, nseq, wts, layer, kv_state):
    (w_in, b_in, wt, bt, brows, nw, cw, sbias, w_out, g1, b1, w_up, fcw, w_down, g2, b2) = wts
    m = x.shape[0]
    t = m // nseq
    ml, mo, cv, qb, kb, vb, skt, svt, g = _proj(x, w_in, b_in, wt, bt, layer, ROW_TILE, nseq, kv_state)
    c0 = jnp.zeros((nseq, ML_HEADS, LANES, LANES), F32)
    m0 = jnp.zeros((nseq, 1, LANES), F32)
    hml, c_ext, m_new = _mlstm(ml, mo, g, c0, m0, nw, layer, nseq, MLSTM_CHUNK, MLSTM_CHUNK, MLSTM_SEQS)
    hsb = _sb_prompt(qb, kb, vb, brows, layer, nseq, ATT_TQ, ATT_TK)
    x2, ptail, gtail = _mix_ffn(hml, cv, hsb, x, w_out, cw, g1, b1, w_up, fcw, w_down, g2, b2, layer, ROW_TILE, t)
    last = lambda tail: tail.reshape(nseq, t // ROW_TILE, SUBLANES, -1)[:, -1, SUBLANES - 2:, :]
    states = (c_ext[:, :, :HEAD_DIM, :HEAD_DIM], c_ext[:, :, :HEAD_DIM, HEAD_DIM], m_new[:, 0, :ML_HEADS],
              last(ptail), last(gtail))
    return x2, (skt, svt), states


def _layer_sample(x, nseq, wts, layer, cache_kt, cache_vt, page_table, c_prev, n_prev, m_prev, conv_prev,
                  ffn_prev):
    (w_in, b_in, wt, bt, brows, nw, cw, sbias, w_out, g1, b1, w_up, fcw, w_down, g2, b2) = wts
    m = x.shape[0]
    t = m // nseq
    ml, mo, cv, qb, _, _, sk, sv, g = _proj(x, w_in, b_in, wt, bt, layer, m, nseq)

    pad = lambda a: jnp.pad(a.reshape(nseq, t, -1), ((0, 0), (0, SAMPLE_PAD - t), (0, 0))).reshape(
        nseq * SAMPLE_PAD, -1)
    m0 = jnp.pad(m_prev, ((0, 0), (0, LANES - ML_HEADS)))[:, None, :]
    hml, c_ext, m_new = _mlstm(pad(ml), pad(mo), pad(g), _ext_state(c_prev, n_prev), m0, nw, layer, nseq,
                               SAMPLE_PAD, t, MLSTM_SEQS)
    hml = hml.reshape(nseq, SAMPLE_PAD, ML_WIDTH)[:, :t].reshape(m, ML_WIDTH)

    q3 = qb.reshape(nseq, t, SB_HEADS, HEAD_DIM)
    eye = jnp.eye(SB_HEADS, dtype=BF16)
    qbd = jnp.einsum("bthd,hg->bhtgd", q3, eye).reshape(nseq, SB_HEADS * t, SB_WIDTH)
    bias = jnp.broadcast_to(jnp.repeat(sbias[layer] * LOG2E, t)[:, None], (SB_HEADS * t, PAGE_SIZE))
    new_t = lambda a: jnp.pad(jnp.swapaxes(a.reshape(nseq, t, SB_WIDTH), 1, 2),
                              ((0, 0), (0, 0), (0, PAGE_SIZE - t)))
    hsb = _sb_sample(qbd, bias, new_t(sk), new_t(sv), cache_kt, cache_vt, page_table, layer, t).reshape(
        m, SB_WIDTH)

    x2, p_all, g_all = _mix_ffn(hml, cv, hsb, x, w_out, cw, g1, b1, w_up, fcw, w_down, g2, b2, layer, m, t,
                                _halo(conv_prev, t), _halo(ffn_prev, t))
    last = lambda a: a.reshape(nseq, t, -1)[:, t - 2:, :]
    states = (sk.reshape(nseq, t, SB_HEADS, HEAD_DIM), sv.reshape(nseq, t, SB_HEADS, HEAD_DIM),
              c_ext[:, :, :HEAD_DIM, :HEAD_DIM], c_ext[:, :, :HEAD_DIM, HEAD_DIM], m_new[:, 0, :ML_HEADS],
              last(p_all), last(g_all))
    return x2, states


def _feature_major_pages(cache):
    d, p = cache.shape[:2]
    return jnp.transpose(cache, (0, 1, 3, 4, 2)).reshape(d, p, SB_WIDTH, PAGE_SIZE)


def _token_major_state(kt):
    d, b, _, t = kt.shape
    return jnp.transpose(kt.reshape(d, b, SB_HEADS, HEAD_DIM, t), (0, 1, 4, 2, 3))


def kernel(x_prompt, x_sample, cache_k, cache_v, state_mlstm_c, state_mlstm_n, state_mlstm_m, state_conv,
           state_ffn_conv, page_table, w_in, b_in, mlstm_norm_w, conv_w, sb_bias, w_out, ln1_g, ln1_b, ffn_w_up,
           ffn_conv_w, ffn_w_down, ln2_g, ln2_b):
    bp, tp, _ = x_prompt.shape
    bs, ts, _ = x_sample.shape
    depth = w_in.shape[0]
    w_in_p, b_in_p, wt_p, bt_p, brows = _pack_w_in(w_in, b_in, sb_bias)
    w_out_b = w_out.astype(BF16)
    w_up_b = ffn_w_up.astype(BF16)
    w_down_b = ffn_w_down.astype(BF16)
    cw_p = _pad_rows8(conv_w)
    fcw_p = _pad_rows8(ffn_conv_w)
    ckt = _feature_major_pages(cache_k)
    cvt = _feature_major_pages(cache_v)

    yp = x_prompt.reshape(bp * tp, D_MODEL)
    ys = x_sample.reshape(bs * ts, D_MODEL)
    st_p = [[] for _ in range(5)]
    st_s = [[] for _ in range(7)]
    row3 = lambda a: a[:, None, :]
    wts = (w_in_p, b_in_p, wt_p, bt_p, brows, row3(mlstm_norm_w), cw_p, sb_bias, w_out_b, row3(ln1_g), row3(ln1_b),
           w_up_b, fcw_p, w_down_b, row3(ln2_g), row3(ln2_b))
    kv_p = (jnp.zeros((depth, bp, SB_WIDTH, tp), F32), jnp.zeros((depth, bp, SB_WIDTH, tp), F32))
    for l in range(depth):
        yp, kv_p, new_p = _layer_prompt(yp, bp, wts, l, kv_p)
        ys, new_s = _layer_sample(ys, bs, wts, l, ckt, cvt, page_table, state_mlstm_c[l], state_mlstm_n[l],
                                  state_mlstm_m[l], state_conv[l], state_ffn_conv[l])
        for lst, a in zip(st_p, new_p):
            lst.append(a)
        for lst, a in zip(st_s, new_s):
            lst.append(a)
    outs_p = [_token_major_state(kv_p[0]), _token_major_state(kv_p[1])] + [jnp.stack(s) for s in st_p]
    outs_s = [jnp.stack(s) for s in st_s]
    return (yp.reshape(bp, tp, D_MODEL), ys.reshape(bs, ts, D_MODEL), *outs_p, *outs_s)
```
